```python
import math
import jax
import jax.numpy as jnp
from jax import lax
import numpy as np

D_MODEL = 1024
BATCH = 16
SEQ = 2048
DEPTH = 4

GRID_W = 64
CTX_LEN = 256
EPS = 1e-6
F32 = jnp.float32

A_HEADS = 8
A_KV_HEADS = 2
A_GROUP = A_HEADS // A_KV_HEADS
A_HEAD_DIM = 64
A_WINDOW = 128
A_BLOCK = 128
A_SCALE = A_HEAD_DIM ** -0.5
A_Q_W = A_HEADS * A_HEAD_DIM
A_KV_W = A_KV_HEADS * A_HEAD_DIM
ROPE_THETA = 10000.0
ROPE_AXIS_DIM = A_HEAD_DIM // 2

B_HEADS = 4
B_HEAD_DIM = 128
B_W = B_HEADS * B_HEAD_DIM
B_CONV = 5
B_CHUNK = 64

C_HEADS = 4
C_HEAD_DIM = 128
C_W = C_HEADS * C_HEAD_DIM
C_CHUNK = 64

N_BRANCH = 3
BRANCH_WIDTHS = (A_Q_W, B_W, C_W)
MIX_W = A_Q_W + B_W + C_W
D_FF = 2816
N_MOD = 9

IN_WIDTHS = (A_Q_W, A_KV_W, A_KV_W,
             3 * B_W, B_W, 2 * B_HEADS, 2 * B_HEADS,
             C_W, 2 * C_W, C_W, C_W,
             N_BRANCH * D_MODEL)
IN_W = sum(IN_WIDTHS)

kernel_name = "hybrid_gqa_deltanet_hgrn2_dit_block"


def _split(h, widths, axis=-1):
    idx = [int(i) for i in np.cumsum(widths)[:-1]]
    return jnp.split(h, idx, axis=axis)


def _rmsnorm(x, g):
    xf = x.astype(F32)
    y = xf * lax.rsqrt(jnp.mean(xf * xf, axis=-1, keepdims=True) + EPS)
    return (y * g.astype(F32)).astype(x.dtype)


def _l2norm(x):
    return x * lax.rsqrt(jnp.sum(x * x, axis=-1, keepdims=True) + EPS)


def _modulate(x, shift, scale):
    return x * (1.0 + scale) + shift


def _swiglu(h, w_gu, w_d):
    gate, up = jnp.split(h @ w_gu, 2, axis=-1)
    return (jax.nn.silu(gate) * up) @ w_d


def _to_heads(a, n_heads):
    B, T = a.shape[:2]
    return a.reshape(B, T, n_heads, -1).transpose(0, 2, 1, 3)


def _axial_rope(T):
    rows = T // GRID_W
    row_pos = jnp.repeat(jnp.arange(rows, dtype=F32), GRID_W)
    col_pos = jnp.tile(jnp.arange(GRID_W, dtype=F32), rows)
    inv = ROPE_THETA ** (-jnp.arange(0, ROPE_AXIS_DIM, 2, dtype=F32) / ROPE_AXIS_DIM)
    ang = jnp.concatenate([row_pos[:, None] * inv, col_pos[:, None] * inv], axis=-1)
    return jnp.cos(ang), jnp.sin(ang)


def _apply_rope(x, cos, sin):
    xf = x.astype(F32)
    x1, x2 = jnp.split(xf, 2, axis=-1)
    cs, sn = cos[None, :, None, :], sin[None, :, None, :]
    return jnp.concatenate([x1 * cs - x2 * sn, x1 * sn + x2 * cs], axis=-1).astype(x.dtype)


def _centred_conv(x, w):
    p = w.shape[0] // 2
    return lax.conv_general_dilated(x, w[:, None, :].astype(x.dtype), (1,), [(p, p)],
                                    dimension_numbers=('NWC', 'WIO', 'NWC'),
                                    feature_group_count=x.shape[-1])


def _sink_column(sink, shape):
    return jnp.broadcast_to(sink.astype(F32).reshape(1, A_KV_HEADS, A_GROUP, 1, 1), shape[:-1] + (1,))


def _window_attention(ql, kl, vl, kc, vc, sink):
    B, T = ql.shape[:2]
    L = kc.shape[1]
    nk = A_BLOCK + 2 * A_WINDOW
    qg = ql.reshape(B, T, A_KV_HEADS, A_GROUP, A_HEAD_DIM)
    pad = ((0, 0), (A_WINDOW, A_WINDOW), (0, 0), (0, 0))
    kp, vp = jnp.pad(kl, pad), jnp.pad(vl, pad)

    def block(n):
        start = n * A_BLOCK
        q = lax.dynamic_slice_in_dim(qg, start, A_BLOCK, axis=1)
        k = lax.dynamic_slice_in_dim(kp, start, nk, axis=1)
        v = lax.dynamic_slice_in_dim(vp, start, nk, axis=1)
        qi = start + jnp.arange(A_BLOCK)
        kj = start - A_WINDOW + jnp.arange(nk)
        valid = (jnp.abs(qi[:, None] - kj[None, :]) <= A_WINDOW) & (kj >= 0) & (kj < T)
        s_loc = jnp.einsum('bqkgd,bskd->bkgqs', q, k, preferred_element_type=F32) * A_SCALE
        s_loc = jnp.where(valid, s_loc, -jnp.inf)
        s_ctx = jnp.einsum('bqkgd,bskd->bkgqs', q, kc, preferred_element_type=F32) * A_SCALE
        s = jnp.concatenate([s_loc, s_ctx, _sink_column(sink, s_loc.shape)], axis=-1)
        p = jax.nn.softmax(s, axis=-1).astype(vl.dtype)
        o = (jnp.einsum('bkgqs,bskd->bqkgd', p[..., :nk], v)
             + jnp.einsum('bkgqs,bskd->bqkgd', p[..., nk:nk + L], vc))
        return o.reshape(B, A_BLOCK, A_Q_W)

    o = lax.map(block, jnp.arange(T // A_BLOCK))
    return jnp.moveaxis(o, 0, 1).reshape(B, T, A_Q_W)


def _context_attention(qc, kc, vc, sink):
    B, L = qc.shape[:2]
    qg = qc.reshape(B, L, A_KV_HEADS, A_GROUP, A_HEAD_DIM)
    s = jnp.einsum('bqkgd,bskd->bkgqs', qg, kc, preferred_element_type=F32) * A_SCALE
    p = jax.nn.softmax(jnp.concatenate([s, _sink_column(sink, s.shape)], axis=-1), axis=-1)
    p = p[..., :L].astype(vc.dtype)
    return jnp.einsum('bkgqs,bskd->bqkgd', p, vc).reshape(B, L, A_Q_W)


def _attention_branch(pc, pl, qk_norm, sink, cos, sin, with_ctx_out):
    def heads(q, k, v):
        B, T = q.shape[:2]
        q = _rmsnorm(q.reshape(B, T, A_HEADS, A_HEAD_DIM), qk_norm[0])
        k = _rmsnorm(k.reshape(B, T, A_KV_HEADS, A_HEAD_DIM), qk_norm[1])
        return q, k, v.reshape(B, T, A_KV_HEADS, A_HEAD_DIM)

    qc, kc, vc = heads(*pc)
    ql, kl, vl = heads(*pl)
    ql, kl = _apply_rope(ql, cos, sin), _apply_rope(kl, cos, sin)
    ol = _window_attention(ql, kl, vl, kc, vc, sink)
    oc = _context_attention(qc, kc, vc, sink) if with_ctx_out else None
    return oc, ol


def _gated_delta_chunked(q, k, v, g, beta, S0):
    B, H, T, _ = q.shape
    Dv = v.shape[-1]
    C = B_CHUNK
    N = T // C
    q, k, v = (a.reshape(B, H, N, C, a.shape[-1]) for a in (q, k, v))
    g = jnp.cumsum(g.reshape(B, H, N, C), axis=-1)
    beta = beta.reshape(B, H, N, C)
    causal = jnp.tril(jnp.ones((C, C), bool))
    strict = jnp.tril(jnp.ones((C, C), bool), -1)
    decay = jnp.exp(jnp.where(causal, g[..., :, None] - g[..., None, :], -jnp.inf))
    kb = k * beta[..., None]
    low = jnp.where(strict, jnp.einsum('bhnid,bhnjd->bhnij', kb, k) * decay, 0.0)
    M = low + jnp.eye(C, dtype=low.dtype)
    rhs = jnp.concatenate([v * beta[..., None], kb * jnp.exp(g)[..., None]], axis=-1)
    sol = lax.linalg.triangular_solve(M, rhs, left_side=True, lower=True, unit_diagonal=True)
    u, w = sol[..., :Dv], sol[..., Dv:]
    a_intra = jnp.where(causal, jnp.einsum('bhnid,bhnjd->bhnij', q, k) * decay, 0.0)

    def step(S, xs):
        qn, kn, un, wn, gn, an = xs
        v_new = un - jnp.einsum('bhck,bhkv->bhcv', wn, S)
        o = (jnp.einsum('bhck,bhkv->bhcv', qn * jnp.exp(gn)[..., None], S)
             + jnp.einsum('bhij,bhjv->bhiv', an, v_new))
        g_last = gn[..., -1:]
        S = (S * jnp.exp(g_last)[..., None]
             + jnp.einsum('bhck,bhcv->bhkv', kn * jnp.exp(g_last - gn)[..., None], v_new))
        return S, o

    xs = tuple(jnp.moveaxis(a, 2, 0) for a in (q, k, u, w, g, a_intra))
    S, o = lax.scan(step, S0, xs)
    return jnp.moveaxis(o, 0, 2).reshape(B, H, T, Dv), S


def _gla_chunked(q, k, v, logf, S0):
    B, H, T, _ = q.shape
    Dv = v.shape[-1]
    C = C_CHUNK
    N = T // C
    q, k, v, logf = (a.reshape(B, H, N, C, a.shape[-1]) for a in (q, k, v, logf))
    b = jnp.cumsum(logf, axis=3)
    causal = jnp.tril(jnp.ones((C, C), bool))[:, :, None]

    def step(S, xs):
        qn, kn, vn, bn = xs
        rel = jnp.exp(jnp.where(causal, bn[:, :, :, None, :] - bn[:, :, None, :, :], -jnp.inf))
        att = jnp.einsum('bhijk,bhjk->bhij', rel * qn[:, :, :, None, :], kn)
        o = (jnp.einsum('bhik,bhkv->bhiv', qn * jnp.exp(bn), S)
             + jnp.einsum('bhij,bhjv->bhiv', att, vn))
        b_last = bn[:, :, -1:, :]
        S = (S * jnp.exp(b_last[:, :, 0, :])[..., None]
             + jnp.einsum('bhjk,bhjv->bhkv', kn * jnp.exp(b_last - bn), vn))
        return S, o

    xs = tuple(jnp.moveaxis(a, 2, 0) for a in (q, k, v, b))
    S, o = lax.scan(step, S0, xs)
    return jnp.moveaxis(o, 0, 2).reshape(B, H, T, Dv), S


def _gated_head_norm(o, z, gain, n_heads):
    B, T = z.shape[:2]
    o = _rmsnorm(o.transpose(0, 2, 1, 3), gain)
    o = o * jax.nn.silu(z.astype(F32).reshape(B, T, n_heads, -1))
    return o.reshape(B, T, -1).astype(z.dtype)


def _flip_t(a, d):
    return jnp.flip(a, axis=2) if d == 1 else a


def _deltanet_branch(pc, pl, conv_w, a_log, dt_bias, norm_g, with_ctx_out):
    def prep(qkv, beta_raw, alpha_raw):
        B, T = qkv.shape[:2]
        qkv = jax.nn.silu(_centred_conv(qkv, conv_w)).astype(F32)
        q, k, v = jnp.split(qkv, 3, axis=-1)
        q = _l2norm(_to_heads(q, B_HEADS)) * (B_HEAD_DIM ** -0.5)
        k = _l2norm(_to_heads(k, B_HEADS))
        v = _to_heads(v, B_HEADS)
        dirs = lambda a: a.astype(F32).reshape(B, T, 2, B_HEADS).transpose(2, 0, 3, 1)
        beta = jax.nn.sigmoid(dirs(beta_raw))
        g = (-jnp.exp(a_log.astype(F32))[:, None, :, None]
             * jax.nn.softplus(dirs(alpha_raw) + dt_bias.astype(F32)[:, None, :, None]))
        return q, k, v, g, beta

    qkv_c, z_c, beta_c, alpha_c = pc
    qkv_l, z_l, beta_l, alpha_l = pl
    qc, kc, vc, gc, bc = prep(qkv_c, beta_c, alpha_c)
    ql, kl, vl, gl, bl = prep(qkv_l, beta_l, alpha_l)
    outs_c, outs_l = [], []
    for d in range(2):
        S0 = jnp.zeros(qc.shape[:2] + (B_HEAD_DIM, B_HEAD_DIM), F32)
        oc, S = _gated_delta_chunked(_flip_t(qc, d), _flip_t(kc, d), _flip_t(vc, d),
                                     _flip_t(gc[d], d), _flip_t(bc[d], d), S0)
        ol, _ = _gated_delta_chunked(_flip_t(ql, d), _flip_t(kl, d), _flip_t(vl, d),
                                     _flip_t(gl[d], d), _flip_t(bl[d], d), S)
        outs_c.append(_flip_t(oc, d))
        outs_l.append(_flip_t(ol, d))
    ol = _gated_head_norm(outs_l[0] + outs_l[1], z_l, norm_g, B_HEADS)
    oc = _gated_head_norm(outs_c[0] + outs_c[1], z_c, norm_g, B_HEADS) if with_ctx_out else None
    return oc, ol


def _hgrn2_branch(pc, pl, lb, norm_g, with_ctx_out):
    lb = lb.astype(F32)[:, None, None, :]

    def prep(q, f_raw, i):
        B, T = q.shape[:2]
        q = _to_heads(jax.nn.silu(q.astype(F32)), C_HEADS) * (C_HEAD_DIM ** -0.5)
        v = _to_heads(i.astype(F32), C_HEADS)
        a = f_raw.astype(F32).reshape(B, T, 2, C_W).transpose(2, 0, 1, 3)
        f = lb + (1.0 - lb) * jax.nn.sigmoid(a)
        k = (1.0 - lb) * jax.nn.sigmoid(-a)
        heads2 = lambda t: t.reshape(2, B, T, C_HEADS, C_HEAD_DIM).transpose(0, 1, 3, 2, 4)
        return q, heads2(k), v, heads2(jnp.log(f))

    q_c, f_c, i_c, o_gate_c = pc
    q_l, f_l, i_l, o_gate_l = pl
    qc, kc, vc, lfc = prep(q_c, f_c, i_c)
    ql, kl, vl, lfl = prep(q_l, f_l, i_l)
    outs_c, outs_l = [], []
    for d in range(2):
        S0 = jnp.zeros(qc.shape[:2] + (C_HEAD_DIM, C_HEAD_DIM), F32)
        oc, S = _gla_chunked(_flip_t(qc, d), _flip_t(kc[d], d), _flip_t(vc, d), _flip_t(lfc[d], d), S0)
        ol, _ = _gla_chunked(_flip_t(ql, d), _flip_t(kl[d], d), _flip_t(vl, d), _flip_t(lfl[d], d), S)
        outs_c.append(_flip_t(oc, d))
        outs_l.append(_flip_t(ol, d))
    ol = _gated_head_norm(outs_l[0] + outs_l[1], o_gate_l, norm_g, C_HEADS)
    oc = _gated_head_norm(outs_c[0] + outs_c[1], o_gate_c, norm_g, C_HEADS) if with_ctx_out else None
    return oc, ol


def _merge(outs, gate_raw, w_branch, w_out):
    gates = jnp.split(jax.nn.sigmoid(gate_raw), N_BRANCH, axis=-1)
    ws = _split(w_branch, BRANCH_WIDTHS, axis=0)
    m = gates[0] * (outs[0] @ ws[0]) + gates[1] * (outs[1] @ ws[1]) + gates[2] * (outs[2] @ ws[2])
    return m @ w_out


def _layer(xc, xl, c_silu, cctx_silu, w_ada, b_ada, norm_g, w_ffn_gu, w_ffn_d, w_in, w_branch, w_out,
           a_qk_norm, a_sink, b_conv, b_a_log, b_dt_bias, b_norm, lb, c_norm, cos, sin, last):
    mod_l = jnp.split((c_silu @ w_ada + b_ada)[:, None, :], N_MOD, axis=-1)
    mod_c = jnp.split(cctx_silu @ w_ada + b_ada, N_MOD, axis=-1)

    def ffn(x, mod, j_norm, j_ffn):
        h = _modulate(_rmsnorm(x, norm_g[j_norm]), mod[3 * j_norm], mod[3 * j_norm + 1])
        return x + 0.5 * mod[3 * j_norm + 2] * _swiglu(h, w_ffn_gu[j_ffn], w_ffn_d[j_ffn])

    xl = ffn(xl, mod_l, 0, 0)
    xc = ffn(xc, mod_c, 0, 0)

    hl = _modulate(_rmsnorm(xl, norm_g[1]), mod_l[3], mod_l[4])
    hc = _modulate(_rmsnorm(xc, norm_g[1]), mod_c[3], mod_c[4])
    pl = _split(hl @ w_in, IN_WIDTHS)
    pc = _split(hc @ w_in, IN_WIDTHS)
    with_ctx_out = not last
    a_oc, a_ol = _attention_branch(pc[0:3], pl[0:3], a_qk_norm, a_sink, cos, sin, with_ctx_out)
    b_oc, b_ol = _deltanet_branch(pc[3:7], pl[3:7], b_conv, b_a_log, b_dt_bias, b_norm, with_ctx_out)
    c_oc, c_ol = _hgrn2_branch(pc[7:11], pl[7:11], lb, c_norm, with_ctx_out)
    xl = xl + mod_l[5] * _merge((a_ol, b_ol, c_ol), pl[11], w_branch, w_out)
    xl = ffn(xl, mod_l, 2, 1)
    if with_ctx_out:
        xc = xc + mod_c[5] * _merge((a_oc, b_oc, c_oc), pc[11], w_branch, w_out)
        xc = ffn(xc, mod_c, 2, 1)
    return xc, xl


def setup_inputs(seed: int = 0) -> dict:
    key = jax.random.key(seed)
    ks = jax.random.split(key, 20)
    D = D_MODEL
    nrm = lambda k, shape, s: jax.random.normal(k, shape, F32) * s
    dt = jnp.exp(jax.random.uniform(ks[16], (DEPTH, 2, B_HEADS), F32, math.log(1e-3), math.log(1e-1)))
    return {
        "x": nrm(ks[0], (BATCH, SEQ, D), 1.0),
        "c": nrm(ks[1], (BATCH, D), 1.0),
        "ctx": nrm(ks[2], (BATCH, CTX_LEN, D), 1.0),
        "c_ctx": nrm(ks[3], (D,), 1.0),
        "w_ada": nrm(ks[4], (DEPTH, D, N_MOD * D), 0.5 * D ** -0.5),
        "b_ada": nrm(ks[5], (DEPTH, N_MOD * D), 0.02),
        "norm_g": 1.0 + nrm(ks[6], (DEPTH, 3, D), 0.02),
        "w_ffn_gu": nrm(ks[7], (DEPTH, 2, D, 2 * D_FF), D ** -0.5),
        "w_ffn_d": nrm(ks[8], (DEPTH, 2, D_FF, D), D_FF ** -0.5),
        "w_in": nrm(ks[9], (DEPTH, D, IN_W), D ** -0.5),
        "w_branch": nrm(ks[10], (DEPTH, MIX_W, D), A_Q_W ** -0.5),
        "w_out": nrm(ks[11], (DEPTH, D, D), D ** -0.5),
        "a_qk_norm": 1.0 + nrm(ks[12], (DEPTH, 2, A_HEAD_DIM), 0.02),
        "a_sink": nrm(ks[13], (DEPTH, A_HEADS), 0.5),
        "b_conv": nrm(ks[14], (DEPTH, B_CONV, 3 * B_W), B_CONV ** -0.5),
        "b_a_log": jnp.log(jax.random.uniform(ks[15], (DEPTH, 2, B_HEADS), F32, 1.0, 16.0)),
        "b_dt_bias": dt + jnp.log(-jnp.expm1(-dt)),
        "b_norm": 1.0 + nrm(ks[17], (DEPTH, B_HEAD_DIM), 0.02),
        "c_lb": nrm(ks[18], (DEPTH, 2, C_W), 0.1),
        "c_norm": 1.0 + nrm(ks[19], (DEPTH, C_HEAD_DIM), 0.02),
    }


def reference(x, c, ctx, c_ctx, w_ada, b_ada, norm_g, w_ffn_gu, w_ffn_d, w_in, w_branch, w_out,
              a_qk_norm, a_sink, b_conv, b_a_log, b_dt_bias, b_norm, c_lb, c_norm):
    T = x.shape[1]
    cos, sin = _axial_rope(T)
    p_lb = jax.nn.softmax(c_lb.astype(F32), axis=0)
    lb_all = jnp.cumsum(p_lb, axis=0) - p_lb[0]
    c_silu = jax.nn.silu(c)
    cctx_silu = jax.nn.silu(c_ctx)
    xl, xc = x, ctx
    for l in range(DEPTH):
        xc, xl = _layer(xc, xl, c_silu, cctx_silu, w_ada[l], b_ada[l], norm_g[l], w_ffn_gu[l], w_ffn_d[l],
                        w_in[l], w_branch[l], w_out[l], a_qk_norm[l], a_sink[l], b_conv[l], b_a_log[l],
                        b_dt_bias[l], b_norm[l], lb_all[l], c_norm[l], cos, sin, l == DEPTH - 1)
    return xl
```

```python
import functools
import math

import numpy as np
import jax
import jax.numpy as jnp
from jax import lax
from jax.experimental import pallas as pl
from jax.experimental.pallas import tpu as pltpu

F32 = jnp.float32
BF16 = jnp.bfloat16
EPS = 1e-6

GRID_W = 64
ROPE_THETA = 10000.0
N_MOD = 9

A_HEADS = 8
A_KV_HEADS = 2
A_HEAD_DIM = 64
A_WINDOW = 128
A_Q_W = A_HEADS * A_HEAD_DIM
A_KV_W = A_KV_HEADS * A_HEAD_DIM
A_QBLK = 256

HEADS = 4
HEAD_DIM = 128
MIX_HW = HEADS * HEAD_DIM
B_CONV = 5
CHUNK = 64
PAIR = 2 * CHUNK
N_LEVELS = 6
DELTA_HALO = 16

LANES = 128
VMEM_LIMIT = 56 * 1024 * 1024


def _dot(a, b):
    return jnp.dot(a, b, preferred_element_type=F32)


def _dot_nt(a, b):
    return lax.dot_general(a, b, (((1,), (1,)), ((), ())), preferred_element_type=F32)


def _dot_tn(a, b):
    return lax.dot_general(a, b, (((0,), (0,)), ((), ())), preferred_element_type=F32)


def _split2(x):
    hi = x.astype(BF16)
    lo = (x - hi.astype(F32)).astype(BF16)
    return hi, lo


def _dot_sel(p, x):
    hi, lo = _split2(x)
    return _dot(p, hi) + _dot(p, lo)


def _silu(x):
    return x * jax.nn.sigmoid(x)


def _softplus(x):
    return jnp.maximum(x, 0.0) + jnp.log(1.0 + jnp.exp(-jnp.abs(x)))


def _rms_rows(x, g):
    return x * lax.rsqrt(jnp.mean(x * x, axis=-1, keepdims=True) + EPS) * g


def _params(sem):
    return pltpu.CompilerParams(dimension_semantics=sem, vmem_limit_bytes=VMEM_LIMIT)


def _ada_kernel(c_ref, w_ref, b_ref, o_ref):
    c = c_ref[...]
    o_ref[...] = _dot(_silu(c).astype(BF16), w_ref[...].astype(BF16)) + b_ref[...]


def _ada_call(c_all, w_ada, b_ada):
    depth, d, _ = w_ada.shape
    rows = c_all.shape[0]
    return pl.pallas_call(
        _ada_kernel,
        grid=(depth, N_MOD),
        in_specs=[
            pl.BlockSpec((rows, d), lambda l, j: (0, 0)),
            pl.BlockSpec((None, d, d), lambda l, j: (l, 0, j)),
            pl.BlockSpec((None, None, 1, d), lambda l, j: (l, j, 0, 0)),
        ],
        out_specs=pl.BlockSpec((None, None, rows, d), lambda l, j: (l, j, 0, 0)),
        out_shape=jax.ShapeDtypeStruct((depth, N_MOD, rows, d), F32),
        compiler_params=_params(("parallel", "parallel")),
    )(c_all, w_ada, b_ada.reshape(depth, N_MOD, 1, d))


def _ffn_kernel(x_ref, mod_ref, g_ref, wg_ref, wu_ref, wd_ref, *rest, j_norm, emit_h, nk):
    if emit_h:
        g2_ref, o_ref, h_ref, hs_ref, acc_ref = rest
    else:
        o_ref, hs_ref, acc_ref = rest
    k = pl.program_id(1)
    m0 = 3 * j_norm

    @pl.when(k == 0)
    def _():
        y = _rms_rows(x_ref[...], g_ref[...])
        hs_ref[...] = (y * (1.0 + mod_ref[m0 + 1:m0 + 2, :]) + mod_ref[m0:m0 + 1, :]).astype(BF16)
        acc_ref[...] = jnp.zeros_like(acc_ref)

    hs = hs_ref[...]
    gate = _dot(hs, wg_ref[...])
    up = _dot(hs, wu_ref[...])
    acc_ref[...] += _dot((_silu(gate) * up).astype(BF16), wd_ref[...])

    @pl.when(k == nk - 1)
    def _():
        out = x_ref[...] + 0.5 * mod_ref[m0 + 2:m0 + 3, :] * acc_ref[...]
        o_ref[...] = out
        if emit_h:
            y = _rms_rows(out, g2_ref[...])
            h_ref[...] = (y * (1.0 + mod_ref[4:5, :]) + mod_ref[3:4, :]).astype(BF16)


def _ffn_call(x, mods, norm_g, w_gu, w_d, *, layer, j_ffn, j_norm, mod_row, emit_h, tm, tf):
    n, d = x.shape
    f = w_d.shape[2]
    nk = f // tf
    in_specs = [
        pl.BlockSpec((tm, d), lambda i, k: (i, 0)),
        pl.BlockSpec((None, None, N_MOD, d), lambda i, k: (layer, mod_row(i), 0, 0)),
        pl.BlockSpec((None, None, 1, d), lambda i, k: (layer, j_norm, 0, 0)),
        pl.BlockSpec((None, None, d, tf), lambda i, k: (layer, j_ffn, 0, k)),
        pl.BlockSpec((None, None, d, tf), lambda i, k: (layer, j_ffn, 0, k + nk)),
        pl.BlockSpec((None, None, tf, d), lambda i, k: (layer, j_ffn, k, 0)),
    ]
    args = [x, mods, norm_g, w_gu, w_gu, w_d]
    out_specs = [pl.BlockSpec((tm, d), lambda i, k: (i, 0))]
    out_shape = [jax.ShapeDtypeStruct((n, d), F32)]
    if emit_h:
        in_specs.append(pl.BlockSpec((None, None, 1, d), lambda i, k: (layer, 1, 0, 0)))
        args.append(norm_g)
        out_specs.append(pl.BlockSpec((tm, d), lambda i, k: (i, 0)))
        out_shape.append(jax.ShapeDtypeStruct((n, d), BF16))
    res = pl.pallas_call(
        functools.partial(_ffn_kernel, j_norm=j_norm, emit_h=emit_h, nk=nk),
        grid=(n // tm, nk),
        in_specs=in_specs,
        out_specs=out_specs,
        out_shape=out_shape,
        scratch_shapes=[pltpu.VMEM((tm, d), BF16), pltpu.VMEM((tm, d), F32)],
        compiler_params=_params(("parallel", "arbitrary")),
    )(*args)
    return res if emit_h else res[0]


def _merge_kernel(x_ref, h_ref, oa_ref, ob_ref, oc_ref, mod_ref, wg_ref, wb_ref, wo_ref, o_ref):
    h = h_ref[...]
    d = x_ref.shape[1]
    m = None
    for i, o_r in enumerate((oa_ref, ob_ref, oc_ref)):
        gate = jax.nn.sigmoid(_dot(h, wg_ref[:, i * d:(i + 1) * d]))
        term = gate * _dot(o_r[...], wb_ref[i * MIX_HW:(i + 1) * MIX_HW, :])
        m = term if m is None else m + term
    y = _dot(m.astype(BF16), wo_ref[...])
    o_ref[...] = x_ref[...] + mod_ref[5:6, :] * y


def _merge_call(x, h, oa, ob, oc, mods, w_gate, w_branch, w_out, *, layer, mod_row, tm):
    n, d = x.shape
    tok = lambda w: pl.BlockSpec((tm, w), lambda i: (i, 0))
    return pl.pallas_call(
        _merge_kernel,
        grid=(n // tm,),
        in_specs=[
            tok(d), tok(d), tok(MIX_HW), tok(MIX_HW), tok(MIX_HW),
            pl.BlockSpec((None, None, N_MOD, d), lambda i: (layer, mod_row(i), 0, 0)),
            pl.BlockSpec((None, d, 3 * d), lambda i: (layer, 0, 0)),
            pl.BlockSpec((None, 3 * MIX_HW, d), lambda i: (layer, 0, 0)),
            pl.BlockSpec((None, d, d), lambda i: (layer, 0, 0)),
        ],
        out_specs=tok(d),
        out_shape=jax.ShapeDtypeStruct((n, d), F32),
        compiler_params=_params(("parallel",)),
    )(x, h, oa, ob, oc, mods, w_gate, w_branch, w_out)


def _group_rms(x, bd, g):
    ss = _dot_sel_rhs(x * x, bd)
    return x * lax.rsqrt(ss * (1.0 / A_HEAD_DIM) + EPS) * g


def _dot_sel_rhs(x, p):
    hi, lo = _split2(x)
    return _dot(hi, p) + _dot(lo, p)


def _rope(x, cos, sin_signed, lane):
    half = A_HEAD_DIM // 2
    partner = jnp.where((lane & (A_HEAD_DIM - 1)) < half,
                        pltpu.roll(x, LANES - half, axis=1), pltpu.roll(x, half, axis=1))
    return x * cos + partner * sin_signed


def _place(x, lane):
    sw = pltpu.roll(x, A_HEAD_DIM, axis=1)
    lo = lane < A_HEAD_DIM
    zero = jnp.zeros_like(x)
    return (jnp.where(lo, x, zero), jnp.where(lo, zero, sw),
            jnp.where(lo, sw, zero), jnp.where(lo, zero, x))


def _attn_kernel(hc_ref, hl_ref, w_ref, gq_ref, gk_ref, sink_ref, cos_ref, sin_ref, bd_ref,
                 oc_ref, ol_ref, kp_s, vp_s, kpc_s, vpc_s, *, T, L):
    bd = bd_ref[...]
    rows = A_QBLK
    lane = lax.broadcasted_iota(jnp.int32, (rows, LANES), 1)

    def kv_tiles(h_rows, cos, sin):
        kv = _dot(h_rows, w_ref[:, A_Q_W:])
        k = _group_rms(kv[:, :A_KV_W], bd, gk_ref[...])
        if cos is not None:
            k = _rope(k, cos, sin, lane)
        return _place(k, lane), _place(kv[:, A_KV_W:], lane)

    for r0 in range(0, L, rows):
        kt, vt = kv_tiles(hc_ref[r0:r0 + rows, :], None, None)
        for j in range(4):
            kpc_s[j, r0:r0 + rows, :] = kt[j].astype(BF16)
            vpc_s[j, r0:r0 + rows, :] = vt[j].astype(BF16)

    zpad = jnp.zeros((A_WINDOW, LANES), BF16)
    for j in range(4):
        kp_s[j, 0:A_WINDOW, :] = zpad
        kp_s[j, A_WINDOW + T:2 * A_WINDOW + T, :] = zpad
        vp_s[j, 0:A_WINDOW, :] = zpad
        vp_s[j, A_WINDOW + T:2 * A_WINDOW + T, :] = zpad

    def kv_body(i, carry):
        r0 = pl.multiple_of(i * rows, rows)
        kt, vt = kv_tiles(hl_ref[pl.ds(r0, rows), :], cos_ref[pl.ds(r0, rows), :], sin_ref[pl.ds(r0, rows), :])
        for j in range(4):
            kp_s[j, pl.ds(r0 + A_WINDOW, rows), :] = kt[j].astype(BF16)
            vp_s[j, pl.ds(r0 + A_WINDOW, rows), :] = vt[j].astype(BF16)
        return carry

    lax.fori_loop(0, T // rows, kv_body, 0)

    def q_tiles(h_rows, cos, sin):
        q = _dot(h_rows, w_ref[:, :A_Q_W])
        out = []
        for t in range(A_Q_W // LANES):
            qt = _group_rms(q[:, t * LANES:(t + 1) * LANES], bd, gq_ref[...])
            if cos is not None:
                qt = _rope(qt, cos, sin, lane)
            out.append(qt.astype(BF16))
        return out

    def attend(qts, k_loc, v_loc, mask_loc, o_ref, o_r0):
        for t, qt in enumerate(qts):
            kvh = t // 2
            acc = None
            for p in range(2):
                head = 2 * t + p
                j = 2 * kvh + p
                sink = sink_ref[head:head + 1, 0:1]
                s_ctx = _dot_nt(qt, kpc_s[j])
                m = jnp.maximum(jnp.max(s_ctx, axis=-1, keepdims=True), sink)
                if k_loc is not None:
                    s_loc = jnp.where(mask_loc, _dot_nt(qt, k_loc(j)), -jnp.inf)
                    m = jnp.maximum(m, jnp.max(s_loc, axis=-1, keepdims=True))
                p_ctx = jnp.exp(s_ctx - m)
                den = jnp.sum(p_ctx, axis=-1, keepdims=True) + jnp.exp(sink - m)
                o = _dot(p_ctx.astype(BF16), vpc_s[j])
                if k_loc is not None:
                    p_loc = jnp.exp(s_loc - m)
                    den = den + jnp.sum(p_loc, axis=-1, keepdims=True)
                    o = o + _dot(p_loc.astype(BF16), v_loc(j))
                o = o * (1.0 / den)
                acc = o if acc is None else acc + o
            o_ref[o_r0, t * LANES:(t + 1) * LANES] = acc.astype(o_ref.dtype)

    for r0 in range(0, L, rows):
        attend(q_tiles(hc_ref[r0:r0 + rows, :], None, None), None, None, None, oc_ref, pl.ds(r0, rows))

    nwin = rows + 2 * A_WINDOW
    r_i = lax.broadcasted_iota(jnp.int32, (rows, nwin), 0)
    c_i = lax.broadcasted_iota(jnp.int32, (rows, nwin), 1)
    band = (c_i >= r_i) & (c_i <= r_i + 2 * A_WINDOW)

    def q_body(i, carry):
        r0 = pl.multiple_of(i * rows, rows)
        mask = band & (c_i >= A_WINDOW - r0) & (c_i < T + A_WINDOW - r0)
        qts = q_tiles(hl_ref[pl.ds(r0, rows), :], cos_ref[pl.ds(r0, rows), :], sin_ref[pl.ds(r0, rows), :])
        attend(qts, lambda j: kp_s[j, pl.ds(r0, nwin), :], lambda j: vp_s[j, pl.ds(r0, nwin), :],
               mask, ol_ref, pl.ds(r0, rows))
        return carry

    lax.fori_loop(0, T // rows, q_body, 0)


def _rope_tables(T):
    rows = T // GRID_W
    half = A_HEAD_DIM // 2
    row_pos = jnp.repeat(jnp.arange(rows, dtype=F32), GRID_W)
    col_pos = jnp.tile(jnp.arange(GRID_W, dtype=F32), rows)
    inv = ROPE_THETA ** (-jnp.arange(0, half, 2, dtype=F32) / half)
    ang = jnp.concatenate([row_pos[:, None] * inv, col_pos[:, None] * inv], axis=-1)
    cos, sin = jnp.cos(ang), jnp.sin(ang)
    cos_t = jnp.tile(cos, (1, LANES // half))
    sin_t = jnp.tile(jnp.concatenate([-sin, sin], axis=-1), (1, LANES // A_HEAD_DIM))
    return cos_t, sin_t


def _attn_call(hc, hl, w_a, gq, gk, sink, cos_t, sin_t, bd, *, layer, B, T, L):
    d = hl.shape[1]
    const = lambda shape: pl.BlockSpec(shape, lambda b: (0,) * len(shape))
    return pl.pallas_call(
        functools.partial(_attn_kernel, T=T, L=L),
        grid=(B,),
        in_specs=[
            pl.BlockSpec((L, d), lambda b: (b, 0)),
            pl.BlockSpec((T, d), lambda b: (b, 0)),
            pl.BlockSpec((None, d, A_Q_W + 2 * A_KV_W), lambda b: (layer, 0, 0)),
            pl.BlockSpec((None, 1, LANES), lambda b: (layer, 0, 0)),
            pl.BlockSpec((None, 1, LANES), lambda b: (layer, 0, 0)),
            pl.BlockSpec((None, A_HEADS, LANES), lambda b: (layer, 0, 0)),
            const((T, LANES)), const((T, LANES)), const((LANES, LANES)),
        ],
        out_specs=[pl.BlockSpec((L, A_Q_W), lambda b: (b, 0)), pl.BlockSpec((T, A_Q_W), lambda b: (b, 0))],
        out_shape=[jax.ShapeDtypeStruct((B * L, A_Q_W), BF16), jax.ShapeDtypeStruct((B * T, A_Q_W), BF16)],
        scratch_shapes=[
            pltpu.VMEM((4, T + 2 * A_WINDOW, LANES), BF16), pltpu.VMEM((4, T + 2 * A_WINDOW, LANES), BF16),
            pltpu.VMEM((4, L, LANES), BF16), pltpu.VMEM((4, L, LANES), BF16),
        ],
        compiler_params=_params(("parallel",)),
    )(hc, hl, w_a, gq, gk, sink, cos_t, sin_t, bd)


def _pair_index(s, npc, npl, reverse):
    if not reverse:
        return s
    return jnp.where(s < npc, npc - 1 - s, 2 * npc + npl - 1 - s)


def _head_norm_store(o_s, gate_of, gain, out_refs, seg_rows, blk):
    base = 0
    for o_ref, nrows in zip(out_refs, seg_rows):
        def body(i, carry, o_ref=o_ref, base=base):
            r0 = pl.multiple_of(i * blk, blk)
            gate = gate_of(base + r0, blk)
            for h in range(HEADS):
                sl = slice(h * HEAD_DIM, (h + 1) * HEAD_DIM)
                o = o_s[pl.ds(base + r0, blk), sl]
                y = _rms_rows(o, gain) * _silu(gate[:, sl])
                o_ref[pl.ds(r0, blk), sl] = y.astype(o_ref.dtype)
            return carry
        lax.fori_loop(0, nrows // blk, body, 0)
        base += nrows


def _delta_kernel(hc_ref, hl_ref, w_ref, cw_ref, nalog_ref, dt_ref, gn_ref, tri_ref,
                  oc_ref, ol_ref, hp_s, qkv_s, ba_s, gt_s, o_s, st_s, *, T, L):
    TL = T + L
    QKV_W = 3 * MIX_HW
    Z0 = QKV_W
    BA0 = QKV_W + MIX_HW
    R = 256
    HALO = DELTA_HALO

    zrow = jnp.zeros((HALO, hp_s.shape[1]), BF16)
    lat0 = L + 2 * HALO
    hp_s[0:HALO, :] = zrow
    hp_s[HALO:HALO + L, :] = hc_ref[...]
    hp_s[HALO + L:lat0 + HALO, :] = jnp.zeros((2 * HALO, hp_s.shape[1]), BF16)
    hp_s[lat0 + HALO:lat0 + HALO + T, :] = hl_ref[...]
    hp_s[lat0 + HALO + T:lat0 + 2 * HALO + T, :] = zrow

    lane_p = lax.broadcasted_iota(jnp.int32, (PAIR, LANES), 1)
    tri_f = tri_ref[0]
    tri_r = tri_ref[1]

    def prep(i, carry):
        r0 = pl.multiple_of(i * R, R)
        p0 = pl.multiple_of(jnp.where(i == 0, 0, r0 + 2 * HALO), HALO)
        hs = hp_s[pl.ds(p0, R + 2 * HALO), :]
        x = _dot(hs, w_ref[:, :QKV_W])
        y = None
        for j in range(B_CONV):
            off = HALO - B_CONV // 2 + j
            term = x[off:off + R, :] * cw_ref[j:j + 1, :]
            y = term if y is None else y + term
        y = _silu(y)
        for t in range(QKV_W // LANES):
            sl = slice(t * LANES, (t + 1) * LANES)
            yt = y[:, sl]
            if t < 2 * HEADS:
                yt = yt * lax.rsqrt(jnp.sum(yt * yt, axis=-1, keepdims=True) + EPS)
                if t < HEADS:
                    yt = yt * (HEAD_DIM ** -0.5)
            qkv_s[pl.ds(r0, R), sl] = yt
        ba = _dot(hs[HALO:HALO + R, :], w_ref[:, BA0:BA0 + LANES])
        beta = jax.nn.sigmoid(ba)
        g = nalog_ref[...] * _softplus(ba + dt_ref[...])
        for u in range(R // PAIR):
            gu = g[u * PAIR:(u + 1) * PAIR, :]
            gcum = jnp.where(lane_p < 8 + HEADS, _dot_sel(tri_f, gu), _dot_sel(tri_r, gu))
            blk = jnp.where(lane_p < 8, beta[u * PAIR:(u + 1) * PAIR, :], gcum)
            ba_s[pl.ds(r0 + u * PAIR, PAIR), :] = blk
            gt_s[i * (R // PAIR) + u] = blk.T
        return carry

    lax.fori_loop(0, TL // R, prep, 0)

    o_s[...] = jnp.zeros_like(o_s)
    st_s[...] = jnp.zeros_like(st_s)

    r_i = lax.broadcasted_iota(jnp.int32, (PAIR, PAIR), 0)
    c_i = lax.broadcasted_iota(jnp.int32, (PAIR, PAIR), 1)
    same = (r_i // CHUNK) == (c_i // CHUNK)
    eye = (r_i == c_i).astype(F32)
    npc, npl = L // PAIR, T // PAIR

    def scan(s, carry):
        for d in range(2):
            p = _pair_index(s, npc, npl, d == 1)
            row0 = pl.multiple_of(p * PAIR, PAIR)
            causal = same & ((c_i <= r_i) if d == 0 else (c_i >= r_i))
            strict = same & ((c_i < r_i) if d == 0 else (c_i > r_i))
            ba = ba_s[pl.ds(row0, PAIR), :]
            gt = gt_s[p]
            for h in range(HEADS):
                q = qkv_s[pl.ds(row0, PAIR), h * HEAD_DIM:(h + 1) * HEAD_DIM]
                k = qkv_s[pl.ds(row0, PAIR), MIX_HW + h * HEAD_DIM:MIX_HW + (h + 1) * HEAD_DIM]
                v = qkv_s[pl.ds(row0, PAIR), 2 * MIX_HW + h * HEAD_DIM:2 * MIX_HW + (h + 1) * HEAD_DIM]
                cb = d * HEADS + h
                beta = ba[:, cb:cb + 1]
                gcol = ba[:, 8 + cb:9 + cb]
                grow = gt[8 + cb:9 + cb, :]
                decay = jnp.exp(jnp.where(causal, gcol - grow, -jnp.inf))
                kb = k * beta
                kbf = k.astype(BF16)
                low = jnp.where(strict, _dot_nt(kb.astype(BF16), kbf) * decay, 0.0)
                a_in = jnp.where(causal, _dot_nt(q.astype(BF16), kbf) * decay, 0.0)
                xk = -low
                inv = eye + xk
                for _ in range(N_LEVELS - 1):
                    xb = xk.astype(BF16)
                    xk = _dot(xb, xb)
                    inv = inv + _dot(inv.astype(BF16), xk.astype(BF16))
                eg = jnp.exp(gcol)
                invb = inv.astype(BF16)
                u_all = _dot(invb, (v * beta).astype(BF16))
                w_all = _dot(invb, (kb * eg).astype(BF16))
                qg = q * eg
                a_b = a_in.astype(BF16)
                st = st_s[cb]
                outs = [None, None]
                for c in ((0, 1) if d == 0 else (1, 0)):
                    cs = slice(c * CHUNK, (c + 1) * CHUNK)
                    last = (c + 1) * CHUNK - 1 if d == 0 else c * CHUNK
                    g_last = gcol[last:last + 1, :]
                    stb = st.astype(BF16)
                    v_new = u_all[cs] - _dot(w_all[cs].astype(BF16), stb)
                    vnb = v_new.astype(BF16)
                    vn_pair = jnp.concatenate([vnb, vnb], axis=0)
                    outs[c] = _dot(qg[cs].astype(BF16), stb) + _dot(a_b[cs, :], vn_pair)
                    kd = k[cs] * jnp.exp(g_last - gcol[cs])
                    st = st * jnp.exp(g_last) + _dot_tn(kd.astype(BF16), vnb)
                st_s[cb] = st
                sl = slice(h * HEAD_DIM, (h + 1) * HEAD_DIM)
                o_s[pl.ds(row0, PAIR), sl] += jnp.concatenate(outs, axis=0)
        return carry

    lax.fori_loop(0, npc + npl, scan, 0)

    def gate_of(r0, blk):
        return _dot(_seq_rows(hc_ref, hl_ref, r0 // blk, blk, L), w_ref[:, Z0:Z0 + MIX_HW])

    _head_norm_store(o_s, gate_of, gn_ref[...], (oc_ref, ol_ref), (L, T), R)


def _delta_call(hc, hl, w_b, conv_w, nalog, dtb, gn, tri, *, layer, B, T, L):
    d = hl.shape[1]
    TL = T + L
    wcols = w_b.shape[2]
    const = lambda shape: pl.BlockSpec(shape, lambda b: (0,) * len(shape))
    return pl.pallas_call(
        functools.partial(_delta_kernel, T=T, L=L),
        grid=(B,),
        in_specs=[
            pl.BlockSpec((L, d), lambda b: (b, 0)),
            pl.BlockSpec((T, d), lambda b: (b, 0)),
            pl.BlockSpec((None, d, wcols), lambda b: (layer, 0, 0)),
            pl.BlockSpec((None, B_CONV, 3 * MIX_HW), lambda b: (layer, 0, 0)),
            pl.BlockSpec((None, 1, LANES), lambda b: (layer, 0, 0)),
            pl.BlockSpec((None, 1, LANES), lambda b: (layer, 0, 0)),
            pl.BlockSpec((None, 1, HEAD_DIM), lambda b: (layer, 0, 0)),
            const((2, PAIR, PAIR)),
        ],
        out_specs=[pl.BlockSpec((L, MIX_HW), lambda b: (b, 0)), pl.BlockSpec((T, MIX_HW), lambda b: (b, 0))],
        out_shape=[jax.ShapeDtypeStruct((B * L, MIX_HW), BF16), jax.ShapeDtypeStruct((B * T, MIX_HW), BF16)],
        scratch_shapes=[
            pltpu.VMEM((TL + 4 * DELTA_HALO, d), BF16),
            pltpu.VMEM((TL, 3 * MIX_HW), F32),
            pltpu.VMEM((TL, LANES), F32),
            pltpu.VMEM((TL // PAIR, LANES, PAIR), F32),
            pltpu.VMEM((TL, MIX_HW), F32),
            pltpu.VMEM((2 * HEADS, HEAD_DIM, HEAD_DIM), F32),
        ],
        compiler_params=_params(("parallel",)),
    )(hc, hl, w_b, conv_w, nalog, dtb, gn, tri)


def _hgrn_kernel(hc_ref, hl_ref, w_ref, clb_ref, gn_ref, pm_ref, role_ref, msk_ref,
                 oc_ref, ol_ref, q_s, v_s, a_s, o_s, st_s, *, T, L, layer):
    TL = T + L
    R = 256
    W = MIX_HW
    npc, npl = L // PAIR, T // PAIR

    clb = clb_ref[...]
    e = jnp.exp(clb - jnp.max(clb, axis=0, keepdims=True))
    tot = jnp.sum(e, axis=0)
    part = jnp.zeros_like(tot)
    for i in range(1, layer + 1):
        part = part + e[i]
    lb = part / tot

    def prep(i, carry):
        r0 = pl.multiple_of(i * R, R)
        hs = _seq_rows(hc_ref, hl_ref, i, R, L)
        x = _dot(hs, w_ref[:, :4 * W])
        q_s[pl.ds(r0, R), :] = _silu(x[:, :W]) * (HEAD_DIM ** -0.5)
        a_s[0, pl.ds(r0, R), :] = x[:, W:2 * W]
        a_s[1, pl.ds(r0, R), :] = x[:, 2 * W:3 * W]
        v_s[pl.ds(r0, R), :] = x[:, 3 * W:4 * W].astype(BF16)
        return carry

    lax.fori_loop(0, TL // R, prep, 0)

    o_s[...] = jnp.zeros_like(o_s)
    st_s[...] = jnp.zeros_like(st_s)
    r_i = lax.broadcasted_iota(jnp.int32, (PAIR, PAIR), 0)
    c_i = lax.broadcasted_iota(jnp.int32, (PAIR, PAIR), 1)
    eye = r_i == c_i

    def scan(s, carry):
        for d in range(2):
            p = _pair_index(s, npc, npl, d == 1)
            row0 = pl.multiple_of(p * PAIR, PAIR)
            a = a_s[d, pl.ds(row0, PAIR), :]
            lbd = lb[d:d + 1, :]
            logf = jnp.log(lbd + (1.0 - lbd) * jax.nn.sigmoid(a))
            kk = (1.0 - lbd) * jax.nn.sigmoid(-a)
            hi, lo = _split2(logf)
            ex = jnp.exp(_dot(pm_ref[d], hi) + _dot(pm_ref[d], lo))
            e_cum = ex[N_LEVELS * PAIR:(N_LEVELS + 1) * PAIR]
            e_rest = ex[(N_LEVELS + 1) * PAIR:(N_LEVELS + 2) * PAIR]
            q = q_s[pl.ds(row0, PAIR), :]
            for h in range(HEADS):
                sl = slice(h * HEAD_DIM, (h + 1) * HEAD_DIM)
                qh, kh, vh = q[:, sl], kk[:, sl], v_s[pl.ds(row0, PAIR), sl]
                att = jnp.where(eye, jnp.sum(qh * kh, axis=-1, keepdims=True), 0.0)
                for lv in range(N_LEVELS):
                    rr = (jnp.where(role_ref[d, lv] > 0.5, qh, kh) * ex[lv * PAIR:(lv + 1) * PAIR, sl]).astype(BF16)
                    att = att + _dot_nt(rr, rr) * msk_ref[d, lv]
                intra = _dot(att.astype(BF16), vh)
                qe = (qh * e_cum[:, sl]).astype(BF16)
                kd = (kh * e_rest[:, sl]).astype(BF16)
                cb = d * HEADS + h
                st = st_s[cb]
                outs = [None, None]
                for c in ((0, 1) if d == 0 else (1, 0)):
                    cs = slice(c * CHUNK, (c + 1) * CHUNK)
                    last = (c + 1) * CHUNK - 1 if d == 0 else c * CHUNK
                    outs[c] = _dot_nt(qe[cs], st.astype(BF16)) + intra[cs]
                    st = st * e_cum[last:last + 1, sl] + _dot_tn(vh[cs], kd[cs])
                st_s[cb] = st
                o_s[pl.ds(row0, PAIR), sl] += jnp.concatenate(outs, axis=0)
        return carry

    lax.fori_loop(0, npc + npl, scan, 0)

    def gate_of(r0, blk):
        return _dot(_seq_rows(hc_ref, hl_ref, r0 // blk, blk, L), w_ref[:, 4 * W:5 * W])

    _head_norm_store(o_s, gate_of, gn_ref[...], (oc_ref, ol_ref), (L, T), R)


def _seq_rows(hc_ref, hl_ref, i, blk, L):
    assert L == blk
    r0 = pl.multiple_of(jnp.maximum(i - 1, 0) * blk, blk)
    return jnp.where(i == 0, hc_ref[...], hl_ref[pl.ds(r0, blk), :])


def _hgrn_call(hc, hl, w_c, c_lb, gn, pm, role, msk, *, layer, B, T, L):
    d = hl.shape[1]
    TL = T + L
    const = lambda shape: pl.BlockSpec(shape, lambda b: (0,) * len(shape))
    return pl.pallas_call(
        functools.partial(_hgrn_kernel, T=T, L=L, layer=layer),
        grid=(B,),
        in_specs=[
            pl.BlockSpec((L, d), lambda b: (b, 0)),
            pl.BlockSpec((T, d), lambda b: (b, 0)),
            pl.BlockSpec((None, d, 5 * MIX_HW), lambda b: (layer, 0, 0)),
            const(c_lb.shape),
            pl.BlockSpec((None, 1, HEAD_DIM), lambda b: (layer, 0, 0)),
            const(pm.shape), const(role.shape), const(msk.shape),
        ],
        out_specs=[pl.BlockSpec((L, MIX_HW), lambda b: (b, 0)), pl.BlockSpec((T, MIX_HW), lambda b: (b, 0))],
        out_shape=[jax.ShapeDtypeStruct((B * L, MIX_HW), BF16), jax.ShapeDtypeStruct((B * T, MIX_HW), BF16)],
        scratch_shapes=[
            pltpu.VMEM((TL, MIX_HW), F32),
            pltpu.VMEM((TL, MIX_HW), BF16),
            pltpu.VMEM((2, TL, MIX_HW), F32),
            pltpu.VMEM((TL, MIX_HW), F32),
            pltpu.VMEM((2 * HEADS, HEAD_DIM, HEAD_DIM), F32),
        ],
        compiler_params=_params(("parallel",)),
    )(hc, hl, w_c, c_lb, gn, pm, role, msk)


def _tri_tables():
    i = np.arange(PAIR)[:, None]
    t = np.arange(PAIR)[None, :]
    same = (i // CHUNK) == (t // CHUNK)
    return np.stack([same & (t <= i), same & (t >= i)]).astype(np.float32)


def _hgrn_tables():
    i = np.arange(PAIR)
    pm = np.zeros((2, N_LEVELS + 2, PAIR, PAIR), np.float32)
    role = np.zeros((2, N_LEVELS, PAIR, 1), np.float32)
    msk = np.zeros((2, N_LEVELS, PAIR, PAIR), np.float32)
    same_chunk = (i[:, None] // CHUNK) == (i[None, :] // CHUNK)
    for d in range(2):
        for lv in range(N_LEVELS):
            s = CHUNK >> (lv + 1)
            mid = (i // (2 * s)) * (2 * s) + s
            is_q = (i >= mid) if d == 0 else (i < mid)
            role[d, lv, :, 0] = is_q
            for r in range(PAIR):
                m = mid[r]
                if d == 0:
                    lo_t, hi_t = (m, r) if is_q[r] else (r + 1, m - 1)
                else:
                    lo_t, hi_t = (r, m - 1) if is_q[r] else (m, r - 1)
                pm[d, lv, r, lo_t:hi_t + 1] = 1.0
            same_blk = (i[:, None] // (2 * s)) == (i[None, :] // (2 * s))
            msk[d, lv] = same_blk & is_q[:, None] & ~is_q[None, :]
        t = i[None, :]
        r = i[:, None]
        pm[d, N_LEVELS] = same_chunk & ((t <= r) if d == 0 else (t >= r))
        pm[d, N_LEVELS + 1] = same_chunk & ((t > r) if d == 0 else (t < r))
    role = np.broadcast_to(role, (2, N_LEVELS, PAIR, HEAD_DIM)).copy()
    return pm.reshape(2, (N_LEVELS + 2) * PAIR, PAIR), role, msk


def kernel(x, c, ctx, c_ctx, w_ada, b_ada, norm_g, w_ffn_gu, w_ffn_d, w_in, w_branch, w_out, a_qk_norm, a_sink, b_conv, b_a_log, b_dt_bias, b_norm, c_lb, c_norm):
    B, T, D = x.shape
    L = ctx.shape[1]
    depth = w_ada.shape[0]
    assert T % A_QBLK == 0 and L == 256 and D % LANES == 0

    n_rows = -(-(B + 1) // 8) * 8
    c_all = jnp.zeros((n_rows, D), F32).at[:B].set(c).at[B].set(c_ctx)
    mods = jnp.swapaxes(_ada_call(c_all, w_ada, b_ada), 1, 2)

    bf = lambda a: a.astype(BF16)
    w_gu, w_d = bf(w_ffn_gu), bf(w_ffn_d)
    o_qkv = A_Q_W + 2 * A_KV_W
    o_b = o_qkv + 4 * MIX_HW
    o_ba = o_b + 4 * HEADS
    o_c = o_ba + 5 * MIX_HW
    w_a = bf(w_in[:, :, :o_qkv])
    w_b = bf(jnp.concatenate([w_in[:, :, o_qkv:o_b],
                              jnp.pad(w_in[:, :, o_b:o_ba], ((0, 0), (0, 0), (0, LANES - 4 * HEADS)))], axis=-1))
    w_c = bf(w_in[:, :, o_ba:o_c])
    w_gate = bf(w_in[:, :, o_c:])
    w_br, w_o = bf(w_branch), bf(w_out)
    norm_g4 = norm_g.reshape(depth, 3, 1, D)

    gq = jnp.tile(a_qk_norm[:, 0:1, :], (1, 1, LANES // A_HEAD_DIM)) * (A_HEAD_DIM ** -0.5)
    gk = jnp.tile(a_qk_norm[:, 1:2, :], (1, 1, LANES // A_HEAD_DIM))
    sink = jnp.broadcast_to(a_sink[:, :, None], (depth, A_HEADS, LANES))
    pad_ba = lambda a: jnp.pad(a.reshape(depth, 1, 2 * HEADS), ((0, 0), (0, 0), (8, LANES - 8 - 2 * HEADS)))
    nalog = pad_ba(-jnp.exp(b_a_log))
    dtb = pad_ba(b_dt_bias)
    cos_t, sin_t = _rope_tables(T)
    bd = jnp.asarray(np.kron(np.eye(LANES // A_HEAD_DIM), np.ones((A_HEAD_DIM, A_HEAD_DIM))), BF16)
    tri = jnp.asarray(_tri_tables(), BF16)
    pm_np, role_np, msk_np = _hgrn_tables()
    pm, role, msk = jnp.asarray(pm_np, BF16), jnp.asarray(role_np, F32), jnp.asarray(msk_np, F32)

    xl = x.reshape(B * T, D)
    xc = ctx.reshape(B * L, D)
    tm_l, tm_c = min(1024, T), min(1024, B * L)
    row_l = lambda tm: (lambda i: (i * tm) // T)
    row_c = lambda tm: (lambda i: B)
    tf = 256

    for l in range(depth):
        last = l == depth - 1
        ffn = functools.partial(_ffn_call, mods=mods, norm_g=norm_g4, w_gu=w_gu, w_d=w_d, layer=l, tf=tf)
        xl, hl = ffn(xl, j_ffn=0, j_norm=0, mod_row=row_l(tm_l), emit_h=True, tm=tm_l)
        xc, hc = ffn(xc, j_ffn=0, j_norm=0, mod_row=row_c(tm_c), emit_h=True, tm=tm_c)
        a_c, a_l = _attn_call(hc, hl, w_a, gq, gk, sink, cos_t, sin_t, bd, layer=l, B=B, T=T, L=L)
        b_c, b_l = _delta_call(hc, hl, w_b, b_conv, nalog, dtb, b_norm.reshape(depth, 1, HEAD_DIM), tri,
                               layer=l, B=B, T=T, L=L)
        c_c, c_l = _hgrn_call(hc, hl, w_c, c_lb, c_norm.reshape(depth, 1, HEAD_DIM), pm, role, msk,
                              layer=l, B=B, T=T, L=L)
        merge = functools.partial(_merge_call, mods=mods, w_gate=w_gate, w_branch=w_br, w_out=w_o, layer=l)
        xl = merge(xl, hl, a_l, b_l, c_l, mod_row=row_l(tm_l // 2), tm=tm_l // 2)
        xl = ffn(xl, j_ffn=1, j_norm=2, mod_row=row_l(tm_l), emit_h=False, tm=tm_l)
        if not last:
            xc = merge(xc, hc, a_c, b_c, c_c, mod_row=row_c(tm_c // 2), tm=tm_c // 2)
            xc = ffn(xc, j_ffn=1, j_norm=2, mod_row=row_c(tm_c), emit_h=False, tm=tm_c)
    return xl.reshape(B, T, D)
```

```python
import functools
import math

import numpy as np
import jax
import jax.numpy as jnp
from jax import lax
from jax.experimental import pallas as pl
from jax.experimental.pallas import tpu as pltpu

F32 = jnp.float32
BF16 = jnp.bfloat16
EPS = 1e-6

GRID_W = 64
ROPE_THETA = 10000.0
N_MOD = 9

A_HEADS = 8
A_KV_HEADS = 2
A_HEAD_DIM = 64
A_WINDOW = 128
A_Q_W = A_HEADS * A_HEAD_DIM
A_KV_W = A_KV_HEADS * A_HEAD_DIM
A_QBLK = 256

HEADS = 4
HEAD_DIM = 128
MIX_HW = HEADS * HEAD_DIM
B_CONV = 5
CHUNK = 64
PAIR = 2 * CHUNK
N_LEVELS = 6
SOLVE_BLK = 16
DELTA_HALO = 16

LANES = 128
VMEM_LIMIT = 56 * 1024 * 1024


def _dot(a, b):
    return jnp.dot(a, b, preferred_element_type=F32)


def _dot_nt(a, b):
    return lax.dot_general(a, b, (((1,), (1,)), ((), ())), preferred_element_type=F32)


def _dot_tn(a, b):
    return lax.dot_general(a, b, (((0,), (0,)), ((), ())), preferred_element_type=F32)


def _bdot(a, b):
    return lax.dot_general(a, b, (((2,), (1,)), ((0,), (0,))), preferred_element_type=F32)


def _bdot_nt(a, b):
    return lax.dot_general(a, b, (((2,), (2,)), ((0,), (0,))), preferred_element_type=F32)


def _bdot_tn(a, b):
    return jnp.stack([_dot_tn(a[i], b[i]) for i in range(a.shape[0])])


def _split2(x):
    hi = x.astype(BF16)
    lo = (x - hi.astype(F32)).astype(BF16)
    return hi, lo


def _dot_sel(p, x):
    hi, lo = _split2(x)
    return _dot(p, hi) + _dot(p, lo)


def _silu(x):
    return x * jax.nn.sigmoid(x)


def _softplus(x):
    return jnp.maximum(x, 0.0) + jnp.log(1.0 + jnp.exp(-jnp.abs(x)))


def _rms_rows(x, g):
    return x * lax.rsqrt(jnp.mean(x * x, axis=-1, keepdims=True) + EPS) * g


def _params(sem):
    return pltpu.CompilerParams(dimension_semantics=sem, vmem_limit_bytes=VMEM_LIMIT)


def _ada_kernel(c_ref, w_ref, b_ref, o_ref):
    c = c_ref[...]
    o_ref[...] = _dot(_silu(c).astype(BF16), w_ref[...].astype(BF16)) + b_ref[...]


def _ada_call(c_all, w_ada, b_ada):
    depth, d, _ = w_ada.shape
    rows = c_all.shape[0]
    return pl.pallas_call(
        _ada_kernel,
        grid=(depth, N_MOD),
        in_specs=[
            pl.BlockSpec((rows, d), lambda l, j: (0, 0)),
            pl.BlockSpec((None, d, d), lambda l, j: (l, 0, j)),
            pl.BlockSpec((None, None, 1, d), lambda l, j: (l, j, 0, 0)),
        ],
        out_specs=pl.BlockSpec((None, None, rows, d), lambda l, j: (l, j, 0, 0)),
        out_shape=jax.ShapeDtypeStruct((depth, N_MOD, rows, d), F32),
        compiler_params=_params(("parallel", "parallel")),
    )(c_all, w_ada, b_ada.reshape(depth, N_MOD, 1, d))


def _ffn_kernel(x_ref, mod_ref, g_ref, wg_ref, wu_ref, wd_ref, *rest, j_norm, emit_h, nk):
    if emit_h:
        g2_ref, o_ref, h_ref, hs_ref, acc_ref = rest
    else:
        o_ref, hs_ref, acc_ref = rest
    k = pl.program_id(1)
    m0 = 3 * j_norm

    @pl.when(k == 0)
    def _():
        y = _rms_rows(x_ref[...], g_ref[...])
        hs_ref[...] = (y * (1.0 + mod_ref[m0 + 1:m0 + 2, :]) + mod_ref[m0:m0 + 1, :]).astype(BF16)
        acc_ref[...] = jnp.zeros_like(acc_ref)

    hs = hs_ref[...]
    gate = _dot(hs, wg_ref[...])
    up = _dot(hs, wu_ref[...])
    acc_ref[...] += _dot((_silu(gate) * up).astype(BF16), wd_ref[...])

    @pl.when(k == nk - 1)
    def _():
        out = x_ref[...] + 0.5 * mod_ref[m0 + 2:m0 + 3, :] * acc_ref[...]
        o_ref[...] = out
        if emit_h:
            y = _rms_rows(out, g2_ref[...])
            h_ref[...] = (y * (1.0 + mod_ref[4:5, :]) + mod_ref[3:4, :]).astype(BF16)


def _ffn_call(x, mods, norm_g, w_gu, w_d, *, layer, j_ffn, j_norm, mod_row, emit_h, tm, tf):
    n, d = x.shape
    f = w_d.shape[2]
    nk = f // tf
    in_specs = [
        pl.BlockSpec((tm, d), lambda i, k: (i, 0)),
        pl.BlockSpec((None, None, N_MOD, d), lambda i, k: (layer, mod_row(i), 0, 0)),
        pl.BlockSpec((None, None, 1, d), lambda i, k: (layer, j_norm, 0, 0)),
        pl.BlockSpec((None, None, d, tf), lambda i, k: (layer, j_ffn, 0, k)),
        pl.BlockSpec((None, None, d, tf), lambda i, k: (layer, j_ffn, 0, k + nk)),
        pl.BlockSpec((None, None, tf, d), lambda i, k: (layer, j_ffn, k, 0)),
    ]
    args = [x, mods, norm_g, w_gu, w_gu, w_d]
    out_specs = [pl.BlockSpec((tm, d), lambda i, k: (i, 0))]
    out_shape = [jax.ShapeDtypeStruct((n, d), F32)]
    if emit_h:
        in_specs.append(pl.BlockSpec((None, None, 1, d), lambda i, k: (layer, 1, 0, 0)))
        args.append(norm_g)
        out_specs.append(pl.BlockSpec((tm, d), lambda i, k: (i, 0)))
        out_shape.append(jax.ShapeDtypeStruct((n, d), BF16))
    res = pl.pallas_call(
        functools.partial(_ffn_kernel, j_norm=j_norm, emit_h=emit_h, nk=nk),
        grid=(n // tm, nk),
        in_specs=in_specs,
        out_specs=out_specs,
        out_shape=out_shape,
        scratch_shapes=[pltpu.VMEM((tm, d), BF16), pltpu.VMEM((tm, d), F32)],
        compiler_params=_params(("parallel", "arbitrary")),
    )(*args)
    return res if emit_h else res[0]


def _merge_kernel(x_ref, h_ref, oa_ref, ob_ref, oc_ref, mod_ref, wg_ref, wb_ref, wo_ref, o_ref):
    h = h_ref[...]
    d = x_ref.shape[1]
    m = None
    for i, o_r in enumerate((oa_ref, ob_ref, oc_ref)):
        gate = jax.nn.sigmoid(_dot(h, wg_ref[:, i * d:(i + 1) * d]))
        term = gate * _dot(o_r[...], wb_ref[i * MIX_HW:(i + 1) * MIX_HW, :])
        m = term if m is None else m + term
    y = _dot(m.astype(BF16), wo_ref[...])
    o_ref[...] = x_ref[...] + mod_ref[5:6, :] * y


def _merge_call(x, h, oa, ob, oc, mods, w_gate, w_branch, w_out, *, layer, mod_row, tm):
    n, d = x.shape
    tok = lambda w: pl.BlockSpec((tm, w), lambda i: (i, 0))
    return pl.pallas_call(
        _merge_kernel,
        grid=(n // tm,),
        in_specs=[
            tok(d), tok(d), tok(MIX_HW), tok(MIX_HW), tok(MIX_HW),
            pl.BlockSpec((None, None, N_MOD, d), lambda i: (layer, mod_row(i), 0, 0)),
            pl.BlockSpec((None, d, 3 * d), lambda i: (layer, 0, 0)),
            pl.BlockSpec((None, 3 * MIX_HW, d), lambda i: (layer, 0, 0)),
            pl.BlockSpec((None, d, d), lambda i: (layer, 0, 0)),
        ],
        out_specs=tok(d),
        out_shape=jax.ShapeDtypeStruct((n, d), F32),
        compiler_params=_params(("parallel",)),
    )(x, h, oa, ob, oc, mods, w_gate, w_branch, w_out)


def _group_rms(x, bd, g):
    ss = _dot_sel_rhs(x * x, bd)
    return x * lax.rsqrt(ss * (1.0 / A_HEAD_DIM) + EPS) * g


def _dot_sel_rhs(x, p):
    hi, lo = _split2(x)
    return _dot(hi, p) + _dot(lo, p)


def _rope(x, cos, sin_signed, lane):
    half = A_HEAD_DIM // 2
    partner = jnp.where((lane & (A_HEAD_DIM - 1)) < half,
                        pltpu.roll(x, LANES - half, axis=1), pltpu.roll(x, half, axis=1))
    return x * cos + partner * sin_signed


def _place(x, lane):
    sw = pltpu.roll(x, A_HEAD_DIM, axis=1)
    lo = lane < A_HEAD_DIM
    zero = jnp.zeros_like(x)
    return (jnp.where(lo, x, zero), jnp.where(lo, zero, sw),
            jnp.where(lo, sw, zero), jnp.where(lo, zero, x))


def _attn_kernel(hc_ref, hl_ref, w_ref, gq_ref, gk_ref, sink_ref, cos_ref, sin_ref, bd_ref,
                 oc_ref, ol_ref, kp_s, vp_s, kpc_s, vpc_s, *, T, L):
    bd = bd_ref[...]
    rows = A_QBLK
    lane = lax.broadcasted_iota(jnp.int32, (rows, LANES), 1)

    def kv_tiles(h_rows, cos, sin):
        kv = _dot(h_rows, w_ref[:, A_Q_W:])
        k = _group_rms(kv[:, :A_KV_W], bd, gk_ref[...])
        if cos is not None:
            k = _rope(k, cos, sin, lane)
        return _place(k, lane), _place(kv[:, A_KV_W:], lane)

    for r0 in range(0, L, rows):
        kt, vt = kv_tiles(hc_ref[r0:r0 + rows, :], None, None)
        for j in range(4):
            kpc_s[j, r0:r0 + rows, :] = kt[j].astype(BF16)
            vpc_s[j, r0:r0 + rows, :] = vt[j].astype(BF16)

    zpad = jnp.zeros((A_WINDOW, LANES), BF16)
    for j in range(4):
        kp_s[j, 0:A_WINDOW, :] = zpad
        kp_s[j, A_WINDOW + T:2 * A_WINDOW + T, :] = zpad
        vp_s[j, 0:A_WINDOW, :] = zpad
        vp_s[j, A_WINDOW + T:2 * A_WINDOW + T, :] = zpad

    def kv_body(i, carry):
        r0 = pl.multiple_of(i * rows, rows)
        kt, vt = kv_tiles(hl_ref[pl.ds(r0, rows), :], cos_ref[pl.ds(r0, rows), :], sin_ref[pl.ds(r0, rows), :])
        for j in range(4):
            kp_s[j, pl.ds(r0 + A_WINDOW, rows), :] = kt[j].astype(BF16)
            vp_s[j, pl.ds(r0 + A_WINDOW, rows), :] = vt[j].astype(BF16)
        return carry

    lax.fori_loop(0, T // rows, kv_body, 0)

    def q_tiles(h_rows, cos, sin):
        q = _dot(h_rows, w_ref[:, :A_Q_W])
        out = []
        for t in range(A_Q_W // LANES):
            qt = _group_rms(q[:, t * LANES:(t + 1) * LANES], bd, gq_ref[...])
            if cos is not None:
                qt = _rope(qt, cos, sin, lane)
            out.append(qt.astype(BF16))
        return out

    def attend(qts, k_loc, v_loc, mask_loc, o_ref, o_r0):
        for t, qt in enumerate(qts):
            kvh = t // 2
            acc = None
            for p in range(2):
                head = 2 * t + p
                j = 2 * kvh + p
                sink = sink_ref[head:head + 1, 0:1]
                s_ctx = _dot_nt(qt, kpc_s[j])
                m = jnp.maximum(jnp.max(s_ctx, axis=-1, keepdims=True), sink)
                if k_loc is not None:
                    s_loc = jnp.where(mask_loc, _dot_nt(qt, k_loc(j)), -jnp.inf)
                    m = jnp.maximum(m, jnp.max(s_loc, axis=-1, keepdims=True))
                p_ctx = jnp.exp(s_ctx - m)
                den = jnp.sum(p_ctx, axis=-1, keepdims=True) + jnp.exp(sink - m)
                o = _dot(p_ctx.astype(BF16), vpc_s[j])
                if k_loc is not None:
                    p_loc = jnp.exp(s_loc - m)
                    den = den + jnp.sum(p_loc, axis=-1, keepdims=True)
                    o = o + _dot(p_loc.astype(BF16), v_loc(j))
                o = o * (1.0 / den)
                acc = o if acc is None else acc + o
            o_ref[o_r0, t * LANES:(t + 1) * LANES] = acc.astype(o_ref.dtype)

    for r0 in range(0, L, rows):
        attend(q_tiles(hc_ref[r0:r0 + rows, :], None, None), None, None, None, oc_ref, pl.ds(r0, rows))

    nwin = rows + 2 * A_WINDOW
    r_i = lax.broadcasted_iota(jnp.int32, (rows, nwin), 0)
    c_i = lax.broadcasted_iota(jnp.int32, (rows, nwin), 1)
    band = (c_i >= r_i) & (c_i <= r_i + 2 * A_WINDOW)

    def q_body(i, carry):
        r0 = pl.multiple_of(i * rows, rows)
        mask = band & (c_i >= A_WINDOW - r0) & (c_i < T + A_WINDOW - r0)
        qts = q_tiles(hl_ref[pl.ds(r0, rows), :], cos_ref[pl.ds(r0, rows), :], sin_ref[pl.ds(r0, rows), :])
        attend(qts, lambda j: kp_s[j, pl.ds(r0, nwin), :], lambda j: vp_s[j, pl.ds(r0, nwin), :],
               mask, ol_ref, pl.ds(r0, rows))
        return carry

    lax.fori_loop(0, T // rows, q_body, 0)


def _rope_tables(T):
    rows = T // GRID_W
    half = A_HEAD_DIM // 2
    row_pos = jnp.repeat(jnp.arange(rows, dtype=F32), GRID_W)
    col_pos = jnp.tile(jnp.arange(GRID_W, dtype=F32), rows)
    inv = ROPE_THETA ** (-jnp.arange(0, half, 2, dtype=F32) / half)
    ang = jnp.concatenate([row_pos[:, None] * inv, col_pos[:, None] * inv], axis=-1)
    cos, sin = jnp.cos(ang), jnp.sin(ang)
    cos_t = jnp.tile(cos, (1, LANES // half))
    sin_t = jnp.tile(jnp.concatenate([-sin, sin], axis=-1), (1, LANES // A_HEAD_DIM))
    return cos_t, sin_t


def _attn_call(hc, hl, w_a, gq, gk, sink, cos_t, sin_t, bd, *, layer, B, T, L):
    d = hl.shape[1]
    const = lambda shape: pl.BlockSpec(shape, lambda b: (0,) * len(shape))
    return pl.pallas_call(
        functools.partial(_attn_kernel, T=T, L=L),
        grid=(B,),
        in_specs=[
            pl.BlockSpec((L, d), lambda b: (b, 0)),
            pl.BlockSpec((T, d), lambda b: (b, 0)),
            pl.BlockSpec((None, d, A_Q_W + 2 * A_KV_W), lambda b: (layer, 0, 0)),
            pl.BlockSpec((None, 1, LANES), lambda b: (layer, 0, 0)),
            pl.BlockSpec((None, 1, LANES), lambda b: (layer, 0, 0)),
            pl.BlockSpec((None, A_HEADS, LANES), lambda b: (layer, 0, 0)),
            const((T, LANES)), const((T, LANES)), const((LANES, LANES)),
        ],
        out_specs=[pl.BlockSpec((L, A_Q_W), lambda b: (b, 0)), pl.BlockSpec((T, A_Q_W), lambda b: (b, 0))],
        out_shape=[jax.ShapeDtypeStruct((B * L, A_Q_W), BF16), jax.ShapeDtypeStruct((B * T, A_Q_W), BF16)],
        scratch_shapes=[
            pltpu.VMEM((4, T + 2 * A_WINDOW, LANES), BF16), pltpu.VMEM((4, T + 2 * A_WINDOW, LANES), BF16),
            pltpu.VMEM((4, L, LANES), BF16), pltpu.VMEM((4, L, LANES), BF16),
        ],
        compiler_params=_params(("parallel",)),
    )(hc, hl, w_a, gq, gk, sink, cos_t, sin_t, bd)


def _pair_index(s, npc, npl, reverse):
    if not reverse:
        return s
    return jnp.where(s < npc, npc - 1 - s, 2 * npc + npl - 1 - s)


def _head_norm_store(o_s, gate_of, gain, out_refs, seg_rows, blk):
    base = 0
    for o_ref, nrows in zip(out_refs, seg_rows):
        def body(i, carry, o_ref=o_ref, base=base):
            r0 = pl.multiple_of(i * blk, blk)
            gate = gate_of(base + r0, blk)
            for h in range(HEADS):
                sl = slice(h * HEAD_DIM, (h + 1) * HEAD_DIM)
                o = o_s[pl.ds(base + r0, blk), sl]
                y = _rms_rows(o, gain) * _silu(gate[:, sl])
                o_ref[pl.ds(r0, blk), sl] = y.astype(o_ref.dtype)
            return carry
        lax.fori_loop(0, nrows // blk, body, 0)
        base += nrows


def _delta_kernel(hc_ref, hl_ref, w_ref, cw_ref, nalog_ref, dt_ref, gn_ref, tri_ref,
                  oc_ref, ol_ref, hp_s, qkv_s, ba_s, gt_s, o_s, st_s, *, T, L):
    TL = T + L
    QKV_W = 3 * MIX_HW
    Z0 = QKV_W
    BA0 = QKV_W + MIX_HW
    R = 256
    HALO = DELTA_HALO

    zrow = jnp.zeros((HALO, hp_s.shape[1]), BF16)
    lat0 = L + 2 * HALO
    hp_s[0:HALO, :] = zrow
    hp_s[HALO:HALO + L, :] = hc_ref[...]
    hp_s[HALO + L:lat0 + HALO, :] = jnp.zeros((2 * HALO, hp_s.shape[1]), BF16)
    hp_s[lat0 + HALO:lat0 + HALO + T, :] = hl_ref[...]
    hp_s[lat0 + HALO + T:lat0 + 2 * HALO + T, :] = zrow

    lane_p = lax.broadcasted_iota(jnp.int32, (PAIR, LANES), 1)
    tri_f = tri_ref[0]
    tri_r = tri_ref[1]

    def prep(i, carry):
        r0 = pl.multiple_of(i * R, R)
        p0 = pl.multiple_of(jnp.where(i == 0, 0, r0 + 2 * HALO), HALO)
        hs = hp_s[pl.ds(p0, R + 2 * HALO), :]
        x = _dot(hs, w_ref[:, :QKV_W])
        y = None
        for j in range(B_CONV):
            off = HALO - B_CONV // 2 + j
            term = x[off:off + R, :] * cw_ref[j:j + 1, :]
            y = term if y is None else y + term
        y = _silu(y)
        for t in range(QKV_W // LANES):
            sl = slice(t * LANES, (t + 1) * LANES)
            yt = y[:, sl]
            if t < 2 * HEADS:
                yt = yt * lax.rsqrt(jnp.sum(yt * yt, axis=-1, keepdims=True) + EPS)
                if t < HEADS:
                    yt = yt * (HEAD_DIM ** -0.5)
            qkv_s[pl.ds(r0, R), sl] = yt
        ba = _dot(hs[HALO:HALO + R, :], w_ref[:, BA0:BA0 + LANES])
        beta = jax.nn.sigmoid(ba)
        g = nalog_ref[...] * _softplus(ba + dt_ref[...])
        for u in range(R // PAIR):
            gu = g[u * PAIR:(u + 1) * PAIR, :]
            gcum = jnp.where(lane_p < 8 + HEADS, _dot_sel(tri_f, gu), _dot_sel(tri_r, gu))
            blk = jnp.where(lane_p < 8, beta[u * PAIR:(u + 1) * PAIR, :], gcum)
            ba_s[pl.ds(r0 + u * PAIR, PAIR), :] = blk
            gt_s[i * (R // PAIR) + u] = blk.T
        return carry

    lax.fori_loop(0, TL // R, prep, 0)

    o_s[...] = jnp.zeros_like(o_s)
    st_s[...] = jnp.zeros_like(st_s)

    r_i = lax.broadcasted_iota(jnp.int32, (PAIR, PAIR), 0)
    c_i = lax.broadcasted_iota(jnp.int32, (PAIR, PAIR), 1)
    same = (r_i // CHUNK) == (c_i // CHUNK)
    eye = (r_i == c_i).astype(F32)
    npc, npl = L // PAIR, T // PAIR

    dh = [(d, h) for d in range(2) for h in range(HEADS)]
    causal_m = jnp.stack([(same & ((c_i <= r_i) if d == 0 else (c_i >= r_i))).astype(F32) for d, _ in dh])
    off_diag = 1.0 - eye
    diag_blk = ((r_i // SOLVE_BLK) == (c_i // SOLVE_BLK)).astype(F32)

    def by_visit(x):
        lo, hi = x[:, :CHUNK], x[:, CHUNK:]
        return (jnp.concatenate([lo[:HEADS], hi[HEADS:]], axis=0), jnp.concatenate([hi[:HEADS], lo[HEADS:]], axis=0))

    def scan(s, carry):
        p_r = _pair_index(s, npc, npl, True)
        rows = (pl.multiple_of(s * PAIR, PAIR), pl.multiple_of(p_r * PAIR, PAIR))
        bas = (ba_s[pl.ds(rows[0], PAIR), :], ba_s[pl.ds(rows[1], PAIR), :])
        gts = (gt_s[s], gt_s[p_r])

        def grab(off):
            return jnp.stack([qkv_s[pl.ds(rows[d], PAIR), off + h * HEAD_DIM:off + (h + 1) * HEAD_DIM] for d, h in dh])

        q, k, v = grab(0), grab(MIX_HW), grab(2 * MIX_HW)
        beta = jnp.stack([bas[d][:, d * HEADS + h:d * HEADS + h + 1] for d, h in dh])
        gcol = jnp.stack([bas[d][:, 8 + d * HEADS + h:9 + d * HEADS + h] for d, h in dh])
        grow = jnp.stack([gts[d][8 + d * HEADS + h:9 + d * HEADS + h, :] for d, h in dh])
        decay = jnp.exp(jnp.where(causal_m > 0.5, gcol - grow, -jnp.inf))
        kb = k * beta
        kbf = k.astype(BF16)
        low = _bdot_nt(kb.astype(BF16), kbf) * decay * off_diag
        a_b = (_bdot_nt(q.astype(BF16), kbf) * decay).astype(BF16)
        xk = -(low * diag_blk)
        dinv = eye + xk
        for _ in range(SOLVE_BLK.bit_length() - 2):
            xb = xk.astype(BF16)
            xk = _bdot(xb, xb)
            dinv = dinv + _bdot(dinv.astype(BF16), xk.astype(BF16))
        yk = -_bdot(dinv.astype(BF16), (low * (1.0 - diag_blk)).astype(BF16))
        ninv = eye + yk
        for _ in range((CHUNK // SOLVE_BLK).bit_length() - 2):
            yb = yk.astype(BF16)
            yk = _bdot(yb, yb)
            ninv = ninv + _bdot(ninv.astype(BF16), yk.astype(BF16))
        inv = _bdot(ninv.astype(BF16), dinv.astype(BF16))
        eg = jnp.exp(gcol)
        invb = inv.astype(BF16)
        u12 = by_visit(_bdot(invb, (v * beta).astype(BF16)))
        w12 = by_visit(_bdot(invb, (kb * eg).astype(BF16)).astype(BF16))
        q12 = by_visit((q * eg).astype(BF16))
        a12 = by_visit(a_b)
        k12 = by_visit(k)
        g12 = by_visit(gcol)
        glast = (jnp.concatenate([gcol[:HEADS, CHUNK - 1:CHUNK], gcol[HEADS:, CHUNK:CHUNK + 1]], axis=0),
                 jnp.concatenate([gcol[:HEADS, PAIR - 1:PAIR], gcol[HEADS:, 0:1]], axis=0))
        st = st_s[...]
        outs = []
        for c in range(2):
            stb = st.astype(BF16)
            vnb = (u12[c] - _bdot(w12[c], stb)).astype(BF16)
            outs.append(_bdot(q12[c], stb) + _bdot(a12[c], jnp.concatenate([vnb, vnb], axis=1)))
            kd = (k12[c] * jnp.exp(glast[c] - g12[c])).astype(BF16)
            st = st * jnp.exp(glast[c]) + _bdot_tn(kd, vnb)
        st_s[...] = st
        for i, (d, h) in enumerate(dh):
            o_pair = jnp.concatenate([outs[0][i], outs[1][i]] if d == 0 else [outs[1][i], outs[0][i]], axis=0)
            o_s[pl.ds(rows[d], PAIR), h * HEAD_DIM:(h + 1) * HEAD_DIM] += o_pair
        return carry

    lax.fori_loop(0, npc + npl, scan, 0)

    def gate_of(r0, blk):
        return _dot(_seq_rows(hc_ref, hl_ref, r0 // blk, blk, L), w_ref[:, Z0:Z0 + MIX_HW])

    _head_norm_store(o_s, gate_of, gn_ref[...], (oc_ref, ol_ref), (L, T), R)


def _delta_call(hc, hl, w_b, conv_w, nalog, dtb, gn, tri, *, layer, B, T, L):
    d = hl.shape[1]
    TL = T + L
    wcols = w_b.shape[2]
    const = lambda shape: pl.BlockSpec(shape, lambda b: (0,) * len(shape))
    return pl.pallas_call(
        functools.partial(_delta_kernel, T=T, L=L),
        grid=(B,),
        in_specs=[
            pl.BlockSpec((L, d), lambda b: (b, 0)),
            pl.BlockSpec((T, d), lambda b: (b, 0)),
            pl.BlockSpec((None, d, wcols), lambda b: (layer, 0, 0)),
            pl.BlockSpec((None, B_CONV, 3 * MIX_HW), lambda b: (layer, 0, 0)),
            pl.BlockSpec((None, 1, LANES), lambda b: (layer, 0, 0)),
            pl.BlockSpec((None, 1, LANES), lambda b: (layer, 0, 0)),
            pl.BlockSpec((None, 1, HEAD_DIM), lambda b: (layer, 0, 0)),
            const((2, PAIR, PAIR)),
        ],
        out_specs=[pl.BlockSpec((L, MIX_HW), lambda b: (b, 0)), pl.BlockSpec((T, MIX_HW), lambda b: (b, 0))],
        out_shape=[jax.ShapeDtypeStruct((B * L, MIX_HW), BF16), jax.ShapeDtypeStruct((B * T, MIX_HW), BF16)],
        scratch_shapes=[
            pltpu.VMEM((TL + 4 * DELTA_HALO, d), BF16),
            pltpu.VMEM((TL, 3 * MIX_HW), F32),
            pltpu.VMEM((TL, LANES), F32),
            pltpu.VMEM((TL // PAIR, LANES, PAIR), F32),
            pltpu.VMEM((TL, MIX_HW), F32),
            pltpu.VMEM((2 * HEADS, HEAD_DIM, HEAD_DIM), F32),
        ],
        compiler_params=_params(("parallel",)),
    )(hc, hl, w_b, conv_w, nalog, dtb, gn, tri)


def _hgrn_kernel(hc_ref, hl_ref, w_ref, clb_ref, gn_ref, pm_ref, role_ref, msk_ref,
                 oc_ref, ol_ref, q_s, v_s, a_s, o_s, st_s, *, T, L, layer):
    TL = T + L
    R = 256
    W = MIX_HW
    npc, npl = L // PAIR, T // PAIR

    clb = clb_ref[...]
    e = jnp.exp(clb - jnp.max(clb, axis=0, keepdims=True))
    tot = jnp.sum(e, axis=0)
    part = jnp.zeros_like(tot)
    for i in range(1, layer + 1):
        part = part + e[i]
    lb = part / tot

    def prep(i, carry):
        r0 = pl.multiple_of(i * R, R)
        hs = _seq_rows(hc_ref, hl_ref, i, R, L)
        x = _dot(hs, w_ref[:, :4 * W])
        q_s[pl.ds(r0, R), :] = _silu(x[:, :W]) * (HEAD_DIM ** -0.5)
        a_s[0, pl.ds(r0, R), :] = x[:, W:2 * W]
        a_s[1, pl.ds(r0, R), :] = x[:, 2 * W:3 * W]
        v_s[pl.ds(r0, R), :] = x[:, 3 * W:4 * W].astype(BF16)
        return carry

    lax.fori_loop(0, TL // R, prep, 0)

    o_s[...] = jnp.zeros_like(o_s)
    st_s[...] = jnp.zeros_like(st_s)
    r_i = lax.broadcasted_iota(jnp.int32, (PAIR, PAIR), 0)
    c_i = lax.broadcasted_iota(jnp.int32, (PAIR, PAIR), 1)
    eye = r_i == c_i

    def heads(x):
        return jnp.stack([x[:, h * HEAD_DIM:(h + 1) * HEAD_DIM] for h in range(HEADS)])

    def scan(s, carry):
        rows = (pl.multiple_of(s * PAIR, PAIR), pl.multiple_of(_pair_index(s, npc, npl, True) * PAIR, PAIR))
        dirs = range(2)
        ex, qh, kh, vh = [], [], [], []
        for d in dirs:
            a = a_s[d, pl.ds(rows[d], PAIR), :]
            lbd = lb[d:d + 1, :]
            logf = jnp.log(lbd + (1.0 - lbd) * jax.nn.sigmoid(a))
            kh.append(heads((1.0 - lbd) * jax.nn.sigmoid(-a)))
            hi, lo = _split2(logf)
            ex.append(jnp.exp(_dot(pm_ref[d], hi) + _dot(pm_ref[d], lo)))
            qh.append(heads(q_s[pl.ds(rows[d], PAIR), :]))
            vh.append(heads(v_s[pl.ds(rows[d], PAIR), :]))
        kind = lambda d, j: heads(ex[d][j * PAIR:(j + 1) * PAIR])
        att = [jnp.where(eye, jnp.sum(qh[d] * kh[d], axis=-1, keepdims=True), 0.0) for d in dirs]
        for lv in range(N_LEVELS):
            for d in dirs:
                rr = (jnp.where(role_ref[d, lv] > 0.5, qh[d], kh[d]) * kind(d, lv)).astype(BF16)
                att[d] = att[d] + _bdot_nt(rr, rr) * msk_ref[d, lv]
        intra = [_bdot(att[d].astype(BF16), vh[d]) for d in dirs]
        e_cum = [kind(d, N_LEVELS) for d in dirs]
        qe = [(qh[d] * e_cum[d]).astype(BF16) for d in dirs]
        kd = [(kh[d] * kind(d, N_LEVELS + 1)).astype(BF16) for d in dirs]
        st = [st_s[d * HEADS:(d + 1) * HEADS] for d in dirs]
        outs = [[None, None], [None, None]]
        for visit in range(2):
            for d in dirs:
                c = visit if d == 0 else 1 - visit
                cs = slice(c * CHUNK, (c + 1) * CHUNK)
                last = (c + 1) * CHUNK - 1 if d == 0 else c * CHUNK
                outs[d][c] = _bdot_nt(qe[d][:, cs], st[d].astype(BF16)) + intra[d][:, cs]
                st[d] = st[d] * e_cum[d][:, last:last + 1] + _bdot_tn(vh[d][:, cs], kd[d][:, cs])
        for d in dirs:
            st_s[d * HEADS:(d + 1) * HEADS] = st[d]
            for h in range(HEADS):
                o_s[pl.ds(rows[d], PAIR), h * HEAD_DIM:(h + 1) * HEAD_DIM] += jnp.concatenate(
                    [outs[d][0][h], outs[d][1][h]], axis=0)
        return carry

    lax.fori_loop(0, npc + npl, scan, 0)

    def gate_of(r0, blk):
        return _dot(_seq_rows(hc_ref, hl_ref, r0 // blk, blk, L), w_ref[:, 4 * W:5 * W])

    _head_norm_store(o_s, gate_of, gn_ref[...], (oc_ref, ol_ref), (L, T), R)


def _seq_rows(hc_ref, hl_ref, i, blk, L):
    assert L == blk
    r0 = pl.multiple_of(jnp.maximum(i - 1, 0) * blk, blk)
    return jnp.where(i == 0, hc_ref[...], hl_ref[pl.ds(r0, blk), :])


def _hgrn_call(hc, hl, w_c, c_lb, gn, pm, role, msk, *, layer, B, T, L):
    d = hl.shape[1]
    TL = T + L
    const = lambda shape: pl.BlockSpec(shape, lambda b: (0,) * len(shape))
    return pl.pallas_call(
        functools.partial(_hgrn_kernel, T=T, L=L, layer=layer),
        grid=(B,),
        in_specs=[
            pl.BlockSpec((L, d), lambda b: (b, 0)),
            pl.BlockSpec((T, d), lambda b: (b, 0)),
            pl.BlockSpec((None, d, 5 * MIX_HW), lambda b: (layer, 0, 0)),
            const(c_lb.shape),
            pl.BlockSpec((None, 1, HEAD_DIM), lambda b: (layer, 0, 0)),
            const(pm.shape), const(role.shape), const(msk.shape),
        ],
        out_specs=[pl.BlockSpec((L, MIX_HW), lambda b: (b, 0)), pl.BlockSpec((T, MIX_HW), lambda b: (b, 0))],
        out_shape=[jax.ShapeDtypeStruct((B * L, MIX_HW), BF16), jax.ShapeDtypeStruct((B * T, MIX_HW), BF16)],
        scratch_shapes=[
            pltpu.VMEM((TL, MIX_HW), F32),
            pltpu.VMEM((TL, MIX_HW), BF16),
            pltpu.VMEM((2, TL, MIX_HW), F32),
            pltpu.VMEM((TL, MIX_HW), F32),
            pltpu.VMEM((2 * HEADS, HEAD_DIM, HEAD_DIM), F32),
        ],
        compiler_params=_params(("parallel",)),
    )(hc, hl, w_c, c_lb, gn, pm, role, msk)


def _tri_tables():
    i = np.arange(PAIR)[:, None]
    t = np.arange(PAIR)[None, :]
    same = (i // CHUNK) == (t // CHUNK)
    return np.stack([same & (t <= i), same & (t >= i)]).astype(np.float32)


def _hgrn_tables():
    i = np.arange(PAIR)
    pm = np.zeros((2, N_LEVELS + 2, PAIR, PAIR), np.float32)
    role = np.zeros((2, N_LEVELS, PAIR, 1), np.float32)
    msk = np.zeros((2, N_LEVELS, PAIR, PAIR), np.float32)
    same_chunk = (i[:, None] // CHUNK) == (i[None, :] // CHUNK)
    for d in range(2):
        for lv in range(N_LEVELS):
            s = CHUNK >> (lv + 1)
            mid = (i // (2 * s)) * (2 * s) + s
            is_q = (i >= mid) if d == 0 else (i < mid)
            role[d, lv, :, 0] = is_q
            for r in range(PAIR):
                m = mid[r]
                if d == 0:
                    lo_t, hi_t = (m, r) if is_q[r] else (r + 1, m - 1)
                else:
                    lo_t, hi_t = (r, m - 1) if is_q[r] else (m, r - 1)
                pm[d, lv, r, lo_t:hi_t + 1] = 1.0
            same_blk = (i[:, None] // (2 * s)) == (i[None, :] // (2 * s))
            msk[d, lv] = same_blk & is_q[:, None] & ~is_q[None, :]
        t = i[None, :]
        r = i[:, None]
        pm[d, N_LEVELS] = same_chunk & ((t <= r) if d == 0 else (t >= r))
        pm[d, N_LEVELS + 1] = same_chunk & ((t > r) if d == 0 else (t < r))
    role = np.broadcast_to(role, (2, N_LEVELS, PAIR, HEAD_DIM)).copy()
    return pm.reshape(2, (N_LEVELS + 2) * PAIR, PAIR), role, msk


def kernel(x, c, ctx, c_ctx, w_ada, b_ada, norm_g, w_ffn_gu, w_ffn_d, w_in, w_branch, w_out, a_qk_norm, a_sink, b_conv, b_a_log, b_dt_bias, b_norm, c_lb, c_norm):
    B, T, D = x.shape
    L = ctx.shape[1]
    depth = w_ada.shape[0]
    assert T % A_QBLK == 0 and L == 256 and D % LANES == 0

    n_rows = -(-(B + 1) // 8) * 8
    c_all = jnp.zeros((n_rows, D), F32).at[:B].set(c).at[B].set(c_ctx)
    mods = jnp.swapaxes(_ada_call(c_all, w_ada, b_ada), 1, 2)

    bf = lambda a: a.astype(BF16)
    w_gu, w_d = bf(w_ffn_gu), bf(w_ffn_d)
    o_qkv = A_Q_W + 2 * A_KV_W
    o_b = o_qkv + 4 * MIX_HW
    o_ba = o_b + 4 * HEADS
    o_c = o_ba + 5 * MIX_HW
    w_a = bf(w_in[:, :, :o_qkv])
    w_b = bf(jnp.concatenate([w_in[:, :, o_qkv:o_b],
                              jnp.pad(w_in[:, :, o_b:o_ba], ((0, 0), (0, 0), (0, LANES - 4 * HEADS)))], axis=-1))
    w_c = bf(w_in[:, :, o_ba:o_c])
    w_gate = bf(w_in[:, :, o_c:])
    w_br, w_o = bf(w_branch), bf(w_out)
    norm_g4 = norm_g.reshape(depth, 3, 1, D)

    gq = jnp.tile(a_qk_norm[:, 0:1, :], (1, 1, LANES // A_HEAD_DIM)) * (A_HEAD_DIM ** -0.5)
    gk = jnp.tile(a_qk_norm[:, 1:2, :], (1, 1, LANES // A_HEAD_DIM))
    sink = jnp.broadcast_to(a_sink[:, :, None], (depth, A_HEADS, LANES))
    pad_ba = lambda a: jnp.pad(a.reshape(depth, 1, 2 * HEADS), ((0, 0), (0, 0), (8, LANES - 8 - 2 * HEADS)))
    nalog = pad_ba(-jnp.exp(b_a_log))
    dtb = pad_ba(b_dt_bias)
    cos_t, sin_t = _rope_tables(T)
    bd = jnp.asarray(np.kron(np.eye(LANES // A_HEAD_DIM), np.ones((A_HEAD_DIM, A_HEAD_DIM))), BF16)
    tri = jnp.asarray(_tri_tables(), BF16)
    pm_np, role_np, msk_np = _hgrn_tables()
    pm, role, msk = jnp.asarray(pm_np, BF16), jnp.asarray(role_np, F32), jnp.asarray(msk_np, F32)

    xl = x.reshape(B * T, D)
    xc = ctx.reshape(B * L, D)
    tm_l, tm_c = min(1024, T), min(1024, B * L)
    row_l = lambda tm: (lambda i: (i * tm) // T)
    row_c = lambda tm: (lambda i: B)
    tf = 256

    for l in range(depth):
        last = l == depth - 1
        ffn = functools.partial(_ffn_call, mods=mods, norm_g=norm_g4, w_gu=w_gu, w_d=w_d, layer=l, tf=tf)
        xl, hl = ffn(xl, j_ffn=0, j_norm=0, mod_row=row_l(tm_l), emit_h=True, tm=tm_l)
        xc, hc = ffn(xc, j_ffn=0, j_norm=0, mod_row=row_c(tm_c), emit_h=True, tm=tm_c)
        a_c, a_l = _attn_call(hc, hl, w_a, gq, gk, sink, cos_t, sin_t, bd, layer=l, B=B, T=T, L=L)
        b_c, b_l = _delta_call(hc, hl, w_b, b_conv, nalog, dtb, b_norm.reshape(depth, 1, HEAD_DIM), tri,
                               layer=l, B=B, T=T, L=L)
        c_c, c_l = _hgrn_call(hc, hl, w_c, c_lb, c_norm.reshape(depth, 1, HEAD_DIM), pm, role, msk,
                              layer=l, B=B, T=T, L=L)
        merge = functools.partial(_merge_call, mods=mods, w_gate=w_gate, w_branch=w_br, w_out=w_o, layer=l)
        xl = merge(xl, hl, a_l, b_l, c_l, mod_row=row_l(tm_l // 2), tm=tm_l // 2)
        xl = ffn(xl, j_ffn=1, j_norm=2, mod_row=row_l(tm_l), emit_h=False, tm=tm_l)
        if not last:
            xc = merge(xc, hc, a_c, b_c, c_c, mod_row=row_c(tm_c // 2), tm=tm_c // 2)
            xc = ffn(xc, j_ffn=1, j_norm=2, mod_row=row_c(tm_c), emit_h=False, tm=tm_c)
    return xl.reshape(B, T, D)
```

```python
import functools
import math

import numpy as np
import jax
import jax.numpy as jnp
from jax import lax
from jax.experimental import pallas as pl
from jax.experimental.pallas import tpu as pltpu

F32 = jnp.float32
BF16 = jnp.bfloat16
EPS = 1e-6

GRID_W = 64
ROPE_THETA = 10000.0
N_MOD = 9

A_HEADS = 8
A_KV_HEADS = 2
A_HEAD_DIM = 64
A_WINDOW = 128
A_Q_W = A_HEADS * A_HEAD_DIM
A_KV_W = A_KV_HEADS * A_HEAD_DIM
A_QBLK = 256

HEADS = 4
HEAD_DIM = 128
MIX_HW = HEADS * HEAD_DIM
B_CONV = 5
CHUNK = 64
PAIR = 2 * CHUNK
N_LEVELS = 6
VPU_LEVELS = 4
DELTA_PREP_ROWS = 256
SOLVE_BLK = 16
DELTA_HALO = 16

LANES = 128
VMEM_LIMIT = 56 * 1024 * 1024


def _dot(a, b):
    return jnp.dot(a, b, preferred_element_type=F32)


def _dot_nt(a, b):
    return lax.dot_general(a, b, (((1,), (1,)), ((), ())), preferred_element_type=F32)


def _dot_tn(a, b):
    return lax.dot_general(a, b, (((0,), (0,)), ((), ())), preferred_element_type=F32)


def _bdot(a, b):
    return lax.dot_general(a, b, (((2,), (1,)), ((0,), (0,))), preferred_element_type=F32)


def _bdot_nt(a, b):
    return lax.dot_general(a, b, (((2,), (2,)), ((0,), (0,))), preferred_element_type=F32)


def _bdot_tn(a, b):
    return jnp.stack([_dot_tn(a[i], b[i]) for i in range(a.shape[0])])


def _split2(x):
    hi = x.astype(BF16)
    lo = (x - hi.astype(F32)).astype(BF16)
    return hi, lo


def _dot_sel(p, x):
    hi, lo = _split2(x)
    return _dot(p, hi) + _dot(p, lo)


def _silu(x):
    return x * jax.nn.sigmoid(x)


def _softplus(x):
    return jnp.maximum(x, 0.0) + jnp.log(1.0 + jnp.exp(-jnp.abs(x)))


def _rms_rows(x, g):
    return x * lax.rsqrt(jnp.mean(x * x, axis=-1, keepdims=True) + EPS) * g


def _params(sem):
    return pltpu.CompilerParams(dimension_semantics=sem, vmem_limit_bytes=VMEM_LIMIT)


def _ada_kernel(c_ref, w_ref, b_ref, o_ref):
    c = c_ref[...]
    o_ref[...] = _dot(_silu(c).astype(BF16), w_ref[...].astype(BF16)) + b_ref[...]


def _ada_call(c_all, w_ada, b_ada):
    depth, d, _ = w_ada.shape
    rows = c_all.shape[0]
    return pl.pallas_call(
        _ada_kernel,
        grid=(depth, N_MOD),
        in_specs=[
            pl.BlockSpec((rows, d), lambda l, j: (0, 0)),
            pl.BlockSpec((None, d, d), lambda l, j: (l, 0, j)),
            pl.BlockSpec((None, None, 1, d), lambda l, j: (l, j, 0, 0)),
        ],
        out_specs=pl.BlockSpec((None, None, rows, d), lambda l, j: (l, j, 0, 0)),
        out_shape=jax.ShapeDtypeStruct((depth, N_MOD, rows, d), F32),
        compiler_params=_params(("parallel", "parallel")),
    )(c_all, w_ada, b_ada.reshape(depth, N_MOD, 1, d))


def _ffn_kernel(x_ref, mod_ref, g_ref, wg_ref, wu_ref, wd_ref, *rest, j_norm, emit_h, nk):
    if emit_h:
        g2_ref, o_ref, h_ref, hs_ref, acc_ref = rest
    else:
        o_ref, hs_ref, acc_ref = rest
    k = pl.program_id(1)
    m0 = 3 * j_norm

    @pl.when(k == 0)
    def _():
        y = _rms_rows(x_ref[...], g_ref[...])
        hs_ref[...] = (y * (1.0 + mod_ref[m0 + 1:m0 + 2, :]) + mod_ref[m0:m0 + 1, :]).astype(BF16)
        acc_ref[...] = jnp.zeros_like(acc_ref)

    hs = hs_ref[...]
    gate = _dot(hs, wg_ref[...])
    up = _dot(hs, wu_ref[...])
    acc_ref[...] += _dot((_silu(gate) * up).astype(BF16), wd_ref[...])

    @pl.when(k == nk - 1)
    def _():
        out = x_ref[...] + 0.5 * mod_ref[m0 + 2:m0 + 3, :] * acc_ref[...]
        o_ref[...] = out
        if emit_h:
            y = _rms_rows(out, g2_ref[...])
            h_ref[...] = (y * (1.0 + mod_ref[4:5, :]) + mod_ref[3:4, :]).astype(BF16)


def _ffn_call(x, mods, norm_g, w_gu, w_d, *, layer, j_ffn, j_norm, mod_row, emit_h, tm, tf):
    n, d = x.shape
    f = w_d.shape[2]
    nk = f // tf
    in_specs = [
        pl.BlockSpec((tm, d), lambda i, k: (i, 0)),
        pl.BlockSpec((None, None, N_MOD, d), lambda i, k: (layer, mod_row(i), 0, 0)),
        pl.BlockSpec((None, None, 1, d), lambda i, k: (layer, j_norm, 0, 0)),
        pl.BlockSpec((None, None, d, tf), lambda i, k: (layer, j_ffn, 0, k)),
        pl.BlockSpec((None, None, d, tf), lambda i, k: (layer, j_ffn, 0, k + nk)),
        pl.BlockSpec((None, None, tf, d), lambda i, k: (layer, j_ffn, k, 0)),
    ]
    args = [x, mods, norm_g, w_gu, w_gu, w_d]
    out_specs = [pl.BlockSpec((tm, d), lambda i, k: (i, 0))]
    out_shape = [jax.ShapeDtypeStruct((n, d), F32)]
    if emit_h:
        in_specs.append(pl.BlockSpec((None, None, 1, d), lambda i, k: (layer, 1, 0, 0)))
        args.append(norm_g)
        out_specs.append(pl.BlockSpec((tm, d), lambda i, k: (i, 0)))
        out_shape.append(jax.ShapeDtypeStruct((n, d), BF16))
    res = pl.pallas_call(
        functools.partial(_ffn_kernel, j_norm=j_norm, emit_h=emit_h, nk=nk),
        grid=(n // tm, nk),
        in_specs=in_specs,
        out_specs=out_specs,
        out_shape=out_shape,
        scratch_shapes=[pltpu.VMEM((tm, d), BF16), pltpu.VMEM((tm, d), F32)],
        compiler_params=_params(("parallel", "arbitrary")),
    )(*args)
    return res if emit_h else res[0]


def _merge_kernel(x_ref, h_ref, oa_ref, ob_ref, oc_ref, mod_ref, wg_ref, wb_ref, wo_ref, o_ref):
    h = h_ref[...]
    d = x_ref.shape[1]
    m = None
    for i, o_r in enumerate((oa_ref, ob_ref, oc_ref)):
        gate = jax.nn.sigmoid(_dot(h, wg_ref[:, i * d:(i + 1) * d]))
        term = gate * _dot(o_r[...], wb_ref[i * MIX_HW:(i + 1) * MIX_HW, :])
        m = term if m is None else m + term
    y = _dot(m.astype(BF16), wo_ref[...])
    o_ref[...] = x_ref[...] + mod_ref[5:6, :] * y


def _merge_call(x, h, oa, ob, oc, mods, w_gate, w_branch, w_out, *, layer, mod_row, tm):
    n, d = x.shape
    tok = lambda w: pl.BlockSpec((tm, w), lambda i: (i, 0))
    return pl.pallas_call(
        _merge_kernel,
        grid=(n // tm,),
        in_specs=[
            tok(d), tok(d), tok(MIX_HW), tok(MIX_HW), tok(MIX_HW),
            pl.BlockSpec((None, None, N_MOD, d), lambda i: (layer, mod_row(i), 0, 0)),
            pl.BlockSpec((None, d, 3 * d), lambda i: (layer, 0, 0)),
            pl.BlockSpec((None, 3 * MIX_HW, d), lambda i: (layer, 0, 0)),
            pl.BlockSpec((None, d, d), lambda i: (layer, 0, 0)),
        ],
        out_specs=tok(d),
        out_shape=jax.ShapeDtypeStruct((n, d), F32),
        compiler_params=_params(("parallel",)),
    )(x, h, oa, ob, oc, mods, w_gate, w_branch, w_out)


def _group_rms(x, bd, g):
    ss = _dot_sel_rhs(x * x, bd)
    return x * lax.rsqrt(ss * (1.0 / A_HEAD_DIM) + EPS) * g


def _dot_sel_rhs(x, p):
    hi, lo = _split2(x)
    return _dot(hi, p) + _dot(lo, p)


def _rope(x, cos, sin_signed, lane):
    half = A_HEAD_DIM // 2
    partner = jnp.where((lane & (A_HEAD_DIM - 1)) < half,
                        pltpu.roll(x, LANES - half, axis=1), pltpu.roll(x, half, axis=1))
    return x * cos + partner * sin_signed


def _place(x, lane):
    sw = pltpu.roll(x, A_HEAD_DIM, axis=1)
    lo = lane < A_HEAD_DIM
    zero = jnp.zeros_like(x)
    return (jnp.where(lo, x, zero), jnp.where(lo, zero, sw),
            jnp.where(lo, sw, zero), jnp.where(lo, zero, x))


def _attn_kernel(hc_ref, hl_ref, w_ref, gq_ref, gk_ref, sink_ref, cos_ref, sin_ref, bd_ref,
                 oc_ref, ol_ref, kp_s, vp_s, kpc_s, vpc_s, *, T, L):
    bd = bd_ref[...]
    rows = A_QBLK
    lane = lax.broadcasted_iota(jnp.int32, (rows, LANES), 1)

    def kv_tiles(h_rows, cos, sin):
        kv = _dot(h_rows, w_ref[:, A_Q_W:])
        k = _group_rms(kv[:, :A_KV_W], bd, gk_ref[...])
        if cos is not None:
            k = _rope(k, cos, sin, lane)
        return _place(k, lane), _place(kv[:, A_KV_W:], lane)

    for r0 in range(0, L, rows):
        kt, vt = kv_tiles(hc_ref[r0:r0 + rows, :], None, None)
        for j in range(4):
            kpc_s[j, r0:r0 + rows, :] = kt[j].astype(BF16)
            vpc_s[j, r0:r0 + rows, :] = vt[j].astype(BF16)

    zpad = jnp.zeros((A_WINDOW, LANES), BF16)
    for j in range(4):
        kp_s[j, 0:A_WINDOW, :] = zpad
        kp_s[j, A_WINDOW + T:2 * A_WINDOW + T, :] = zpad
        vp_s[j, 0:A_WINDOW, :] = zpad
        vp_s[j, A_WINDOW + T:2 * A_WINDOW + T, :] = zpad

    def kv_body(i, carry):
        r0 = pl.multiple_of(i * rows, rows)
        kt, vt = kv_tiles(hl_ref[pl.ds(r0, rows), :], cos_ref[pl.ds(r0, rows), :], sin_ref[pl.ds(r0, rows), :])
        for j in range(4):
            kp_s[j, pl.ds(r0 + A_WINDOW, rows), :] = kt[j].astype(BF16)
            vp_s[j, pl.ds(r0 + A_WINDOW, rows), :] = vt[j].astype(BF16)
        return carry

    lax.fori_loop(0, T // rows, kv_body, 0)

    def q_tiles(h_rows, cos, sin):
        q = _dot(h_rows, w_ref[:, :A_Q_W])
        out = []
        for t in range(A_Q_W // LANES):
            qt = _group_rms(q[:, t * LANES:(t + 1) * LANES], bd, gq_ref[...])
            if cos is not None:
                qt = _rope(qt, cos, sin, lane)
            out.append(qt.astype(BF16))
        return out

    def attend(qts, k_loc, v_loc, mask_loc, o_ref, o_r0):
        def lane_fold(op, x, y=None):
            tiles = [x[:, c:c + LANES] for c in range(0, x.shape[1], LANES)]
            if y is not None:
                tiles += [y[:, c:c + LANES] for c in range(0, y.shape[1], LANES)]
            return functools.reduce(op, tiles)

        for t, qt in enumerate(qts):
            js = [2 * (t // 2) + p for p in range(2)]
            sinks = [sink_ref[2 * t + p:2 * t + p + 1, 0:1] for p in range(2)]
            s_ctx = [_dot_nt(qt, kpc_s[j]) for j in js]
            s_loc = [None, None]
            if k_loc is not None:
                s_loc = [jnp.where(mask_loc, _dot_nt(qt, k_loc(j)), -jnp.inf) for j in js]
            m = [jnp.maximum(jnp.max(lane_fold(jnp.maximum, sc, sl), axis=-1, keepdims=True), sk)
                 for sc, sl, sk in zip(s_ctx, s_loc, sinks)]
            p_ctx = [jnp.exp(sc - mi) for sc, mi in zip(s_ctx, m)]
            p_loc = [None, None]
            if k_loc is not None:
                p_loc = [jnp.exp(sl - mi) for sl, mi in zip(s_loc, m)]
            den = [jnp.sum(lane_fold(jnp.add, pc, pl_), axis=-1, keepdims=True) + jnp.exp(sk - mi)
                   for pc, pl_, sk, mi in zip(p_ctx, p_loc, sinks, m)]
            o = [_dot(pc.astype(BF16), vpc_s[j]) for pc, j in zip(p_ctx, js)]
            if k_loc is not None:
                o = [oi + _dot(pl_.astype(BF16), v_loc(j)) for oi, pl_, j in zip(o, p_loc, js)]
            acc = o[0] * (1.0 / den[0]) + o[1] * (1.0 / den[1])
            o_ref[o_r0, t * LANES:(t + 1) * LANES] = acc.astype(o_ref.dtype)

    for r0 in range(0, L, rows):
        attend(q_tiles(hc_ref[r0:r0 + rows, :], None, None), None, None, None, oc_ref, pl.ds(r0, rows))

    nwin = rows + 2 * A_WINDOW
    r_i = lax.broadcasted_iota(jnp.int32, (rows, nwin), 0)
    c_i = lax.broadcasted_iota(jnp.int32, (rows, nwin), 1)
    band = (c_i >= r_i) & (c_i <= r_i + 2 * A_WINDOW)

    def q_body(i, carry):
        r0 = pl.multiple_of(i * rows, rows)
        mask = band & (c_i >= A_WINDOW - r0) & (c_i < T + A_WINDOW - r0)
        qts = q_tiles(hl_ref[pl.ds(r0, rows), :], cos_ref[pl.ds(r0, rows), :], sin_ref[pl.ds(r0, rows), :])
        attend(qts, lambda j: kp_s[j, pl.ds(r0, nwin), :], lambda j: vp_s[j, pl.ds(r0, nwin), :],
               mask, ol_ref, pl.ds(r0, rows))
        return carry

    lax.fori_loop(0, T // rows, q_body, 0)


def _rope_tables(T):
    rows = T // GRID_W
    half = A_HEAD_DIM // 2
    row_pos = jnp.repeat(jnp.arange(rows, dtype=F32), GRID_W)
    col_pos = jnp.tile(jnp.arange(GRID_W, dtype=F32), rows)
    inv = ROPE_THETA ** (-jnp.arange(0, half, 2, dtype=F32) / half)
    ang = jnp.concatenate([row_pos[:, None] * inv, col_pos[:, None] * inv], axis=-1)
    cos, sin = jnp.cos(ang), jnp.sin(ang)
    cos_t = jnp.tile(cos, (1, LANES // half))
    sin_t = jnp.tile(jnp.concatenate([-sin, sin], axis=-1), (1, LANES // A_HEAD_DIM))
    return cos_t, sin_t


def _attn_call(hc, hl, w_a, gq, gk, sink, cos_t, sin_t, bd, *, layer, B, T, L):
    d = hl.shape[1]
    const = lambda shape: pl.BlockSpec(shape, lambda b: (0,) * len(shape))
    return pl.pallas_call(
        functools.partial(_attn_kernel, T=T, L=L),
        grid=(B,),
        in_specs=[
            pl.BlockSpec((L, d), lambda b: (b, 0)),
            pl.BlockSpec((T, d), lambda b: (b, 0)),
            pl.BlockSpec((None, d, A_Q_W + 2 * A_KV_W), lambda b: (layer, 0, 0)),
            pl.BlockSpec((None, 1, LANES), lambda b: (layer, 0, 0)),
            pl.BlockSpec((None, 1, LANES), lambda b: (layer, 0, 0)),
            pl.BlockSpec((None, A_HEADS, LANES), lambda b: (layer, 0, 0)),
            const((T, LANES)), const((T, LANES)), const((LANES, LANES)),
        ],
        out_specs=[pl.BlockSpec((L, A_Q_W), lambda b: (b, 0)), pl.BlockSpec((T, A_Q_W), lambda b: (b, 0))],
        out_shape=[jax.ShapeDtypeStruct((B * L, A_Q_W), BF16), jax.ShapeDtypeStruct((B * T, A_Q_W), BF16)],
        scratch_shapes=[
            pltpu.VMEM((4, T + 2 * A_WINDOW, LANES), BF16), pltpu.VMEM((4, T + 2 * A_WINDOW, LANES), BF16),
            pltpu.VMEM((4, L, LANES), BF16), pltpu.VMEM((4, L, LANES), BF16),
        ],
        compiler_params=_params(("parallel",)),
    )(hc, hl, w_a, gq, gk, sink, cos_t, sin_t, bd)


def _pair_index(s, npc, npl, reverse):
    if not reverse:
        return s
    return jnp.where(s < npc, npc - 1 - s, 2 * npc + npl - 1 - s)


def _head_norm_store(o_s, gate_of, gain, out_refs, seg_rows, blk):
    base = 0
    for o_ref, nrows in zip(out_refs, seg_rows):
        def body(i, carry, o_ref=o_ref, base=base):
            r0 = pl.multiple_of(i * blk, blk)
            gate = gate_of(base + r0, blk)
            for h in range(HEADS):
                sl = slice(h * HEAD_DIM, (h + 1) * HEAD_DIM)
                o = o_s[pl.ds(base + r0, blk), sl]
                y = _rms_rows(o, gain) * _silu(gate[:, sl])
                o_ref[pl.ds(r0, blk), sl] = y.astype(o_ref.dtype)
            return carry
        lax.fori_loop(0, nrows // blk, body, 0)
        base += nrows


def _delta_kernel(hc_ref, hl_ref, w_ref, cw_ref, nalog_ref, dt_ref, gn_ref, tri_ref,
                  oc_ref, ol_ref, hp_s, xc_s, qkv_s, ba_s, gt_s, o_s, st_s, *, T, L):
    TL = T + L
    QKV_W = 3 * MIX_HW
    Z0 = QKV_W
    BA0 = QKV_W + MIX_HW
    R = DELTA_PREP_ROWS
    HALO = DELTA_HALO

    zrow = jnp.zeros((HALO, hp_s.shape[1]), BF16)
    lat0 = L + 2 * HALO
    hp_s[0:HALO, :] = zrow
    hp_s[HALO:HALO + L, :] = hc_ref[...]
    hp_s[HALO + L:lat0 + HALO, :] = jnp.zeros((2 * HALO, hp_s.shape[1]), BF16)
    hp_s[lat0 + HALO:lat0 + HALO + T, :] = hl_ref[...]
    hp_s[lat0 + HALO + T:lat0 + 2 * HALO + T, :] = zrow

    lane_p = lax.broadcasted_iota(jnp.int32, (PAIR, LANES), 1)
    tri_f = tri_ref[0]
    tri_r = tri_ref[1]

    def prep(i, carry):
        r0 = pl.multiple_of(i * R, R)
        p0 = pl.multiple_of(jnp.where(i == 0, 0, r0 + 2 * HALO), HALO)
        hs = hp_s[pl.ds(p0, R + 2 * HALO), :]
        xc_s[...] = _dot(hs, w_ref[:, :QKV_W])
        y = None
        for j in range(B_CONV):
            off = HALO - B_CONV // 2 + j
            term = xc_s[off:off + R, :] * cw_ref[j:j + 1, :]
            y = term if y is None else y + term
        y = _silu(y)
        for t in range(QKV_W // LANES):
            sl = slice(t * LANES, (t + 1) * LANES)
            yt = y[:, sl]
            if t < 2 * HEADS:
                yt = yt * lax.rsqrt(jnp.sum(yt * yt, axis=-1, keepdims=True) + EPS)
                if t < HEADS:
                    yt = yt * (HEAD_DIM ** -0.5)
            qkv_s[pl.ds(r0, R), sl] = yt
        ba = _dot(hs[HALO:HALO + R, :], w_ref[:, BA0:BA0 + LANES])
        beta = jax.nn.sigmoid(ba)
        g = nalog_ref[...] * _softplus(ba + dt_ref[...])
        for u in range(R // PAIR):
            gu = g[u * PAIR:(u + 1) * PAIR, :]
            gcum = jnp.where(lane_p < 8 + HEADS, _dot_sel(tri_f, gu), _dot_sel(tri_r, gu))
            blk = jnp.where(lane_p < 8, beta[u * PAIR:(u + 1) * PAIR, :], gcum)
            ba_s[pl.ds(r0 + u * PAIR, PAIR), :] = blk
            gt_s[i * (R // PAIR) + u] = blk.T
        return carry

    lax.fori_loop(0, TL // R, prep, 0)

    o_s[...] = jnp.zeros_like(o_s)
    st_s[...] = jnp.zeros_like(st_s)

    r_i = lax.broadcasted_iota(jnp.int32, (PAIR, PAIR), 0)
    c_i = lax.broadcasted_iota(jnp.int32, (PAIR, PAIR), 1)
    same = (r_i // CHUNK) == (c_i // CHUNK)
    eye = (r_i == c_i).astype(F32)
    npc, npl = L // PAIR, T // PAIR

    dh = [(d, h) for d in range(2) for h in range(HEADS)]
    causal_m = jnp.stack([(same & ((c_i <= r_i) if d == 0 else (c_i >= r_i))).astype(F32) for d, _ in dh])
    off_diag = 1.0 - eye
    diag_blk = ((r_i // SOLVE_BLK) == (c_i // SOLVE_BLK)).astype(F32)

    def by_visit(x):
        lo, hi = x[:, :CHUNK], x[:, CHUNK:]
        return (jnp.concatenate([lo[:HEADS], hi[HEADS:]], axis=0), jnp.concatenate([hi[:HEADS], lo[HEADS:]], axis=0))

    def scan(s, carry):
        p_r = _pair_index(s, npc, npl, True)
        rows = (pl.multiple_of(s * PAIR, PAIR), pl.multiple_of(p_r * PAIR, PAIR))
        bas = (ba_s[pl.ds(rows[0], PAIR), :], ba_s[pl.ds(rows[1], PAIR), :])
        gts = (gt_s[s], gt_s[p_r])

        def grab(off):
            return jnp.stack([qkv_s[pl.ds(rows[d], PAIR), off + h * HEAD_DIM:off + (h + 1) * HEAD_DIM] for d, h in dh])

        q, k, v = grab(0), grab(MIX_HW), grab(2 * MIX_HW)
        beta = jnp.stack([bas[d][:, d * HEADS + h:d * HEADS + h + 1] for d, h in dh])
        gcol = jnp.stack([bas[d][:, 8 + d * HEADS + h:9 + d * HEADS + h] for d, h in dh])
        grow = jnp.stack([gts[d][8 + d * HEADS + h:9 + d * HEADS + h, :] for d, h in dh])
        decay = jnp.exp(jnp.where(causal_m > 0.5, gcol - grow, -jnp.inf))
        kb = k * beta
        kq =_bdot_nt(jnp.concatenate([kb, q], axis=1).astype(BF16), k.astype(BF16))
        low = kq[:, :PAIR] * decay * off_diag
        a_b = (kq[:, PAIR:] * decay).astype(BF16)
        xk = -(low * diag_blk)
        dinv = eye + xk
        for _ in range(SOLVE_BLK.bit_length() - 2):
            xb = xk.astype(BF16)
            xk = _bdot(xb, xb)
            dinv = dinv + _bdot(dinv.astype(BF16), xk.astype(BF16))
        yk = -_bdot(dinv.astype(BF16), (low * (1.0 - diag_blk)).astype(BF16))
        ninv = eye + yk
        for _ in range((CHUNK // SOLVE_BLK).bit_length() - 2):
            yb = yk.astype(BF16)
            yk = _bdot(yb, yb)
            ninv = ninv + _bdot(ninv.astype(BF16), yk.astype(BF16))
        inv = _bdot(ninv.astype(BF16), dinv.astype(BF16))
        eg = jnp.exp(gcol)
        uw = _bdot(inv.astype(BF16), jnp.concatenate([v * beta, kb * eg], axis=-1).astype(BF16))
        u12 = by_visit(uw[..., :HEAD_DIM])
        w12 = by_visit(uw[..., HEAD_DIM:].astype(BF16))
        q12 = by_visit((q * eg).astype(BF16))
        a12 = by_visit(a_b)
        k12 = by_visit(k)
        g12 = by_visit(gcol)
        glast = (jnp.concatenate([gcol[:HEADS, CHUNK - 1:CHUNK], gcol[HEADS:, CHUNK:CHUNK + 1]], axis=0),
                 jnp.concatenate([gcol[:HEADS, PAIR - 1:PAIR], gcol[HEADS:, 0:1]], axis=0))
        st = st_s[...]
        outs = []
        for c in range(2):
            stb = st.astype(BF16)
            wq_s = _bdot(jnp.concatenate([w12[c], q12[c]], axis=1), stb)
            vnb = (u12[c] - wq_s[:, :CHUNK]).astype(BF16)
            outs.append(wq_s[:, CHUNK:] + _bdot(a12[c], jnp.concatenate([vnb, vnb], axis=1)))
            kd = (k12[c] * jnp.exp(glast[c] - g12[c])).astype(BF16)
            st = st * jnp.exp(glast[c]) + _bdot_tn(kd, vnb)
        st_s[...] = st
        for i, (d, h) in enumerate(dh):
            o_pair = jnp.concatenate([outs[0][i], outs[1][i]] if d == 0 else [outs[1][i], outs[0][i]], axis=0)
            o_s[pl.ds(rows[d], PAIR), h * HEAD_DIM:(h + 1) * HEAD_DIM] += o_pair
        return carry

    lax.fori_loop(0, npc + npl, scan, 0)

    def gate_of(r0, blk):
        return _dot(_seq_rows(hc_ref, hl_ref, r0 // blk, blk, L), w_ref[:, Z0:Z0 + MIX_HW])

    _head_norm_store(o_s, gate_of, gn_ref[...], (oc_ref, ol_ref), (L, T), R)


def _delta_call(hc, hl, w_b, conv_w, nalog, dtb, gn, tri, *, layer, B, T, L):
    d = hl.shape[1]
    TL = T + L
    wcols = w_b.shape[2]
    const = lambda shape: pl.BlockSpec(shape, lambda b: (0,) * len(shape))
    return pl.pallas_call(
        functools.partial(_delta_kernel, T=T, L=L),
        grid=(B,),
        in_specs=[
            pl.BlockSpec((L, d), lambda b: (b, 0)),
            pl.BlockSpec((T, d), lambda b: (b, 0)),
            pl.BlockSpec((None, d, wcols), lambda b: (layer, 0, 0)),
            pl.BlockSpec((None, B_CONV, 3 * MIX_HW), lambda b: (layer, 0, 0)),
            pl.BlockSpec((None, 1, LANES), lambda b: (layer, 0, 0)),
            pl.BlockSpec((None, 1, LANES), lambda b: (layer, 0, 0)),
            pl.BlockSpec((None, 1, HEAD_DIM), lambda b: (layer, 0, 0)),
            const((2, PAIR, PAIR)),
        ],
        out_specs=[pl.BlockSpec((L, MIX_HW), lambda b: (b, 0)), pl.BlockSpec((T, MIX_HW), lambda b: (b, 0))],
        out_shape=[jax.ShapeDtypeStruct((B * L, MIX_HW), BF16), jax.ShapeDtypeStruct((B * T, MIX_HW), BF16)],
        scratch_shapes=[
            pltpu.VMEM((TL + 4 * DELTA_HALO, d), BF16),
            pltpu.VMEM((DELTA_PREP_ROWS + 2 * DELTA_HALO, 3 * MIX_HW), F32),
            pltpu.VMEM((TL, 3 * MIX_HW), F32),
            pltpu.VMEM((TL, LANES), F32),
            pltpu.VMEM((TL // PAIR, LANES, PAIR), F32),
            pltpu.VMEM((TL, MIX_HW), F32),
            pltpu.VMEM((2 * HEADS, HEAD_DIM, HEAD_DIM), F32),
        ],
        compiler_params=_params(("parallel",)),
    )(hc, hl, w_b, conv_w, nalog, dtb, gn, tri)


def _hgrn_kernel(hc_ref, hl_ref, w_ref, clb_ref, gn_ref, pm_ref, role_ref, msk_ref,
                 oc_ref, ol_ref, q_s, v_s, a_s, o_s, st_s, *, T, L, layer):
    TL = T + L
    R = 256
    W = MIX_HW
    npc, npl = L // PAIR, T // PAIR

    clb = clb_ref[...]
    e = jnp.exp(clb - jnp.max(clb, axis=0, keepdims=True))
    tot = jnp.sum(e, axis=0)
    part = jnp.zeros_like(tot)
    for i in range(1, layer + 1):
        part = part + e[i]
    lb = part / tot

    def prep(i, carry):
        r0 = pl.multiple_of(i * R, R)
        hs = _seq_rows(hc_ref, hl_ref, i, R, L)
        x = _dot(hs, w_ref[:, :4 * W])
        q_s[pl.ds(r0, R), :] = _silu(x[:, :W]) * (HEAD_DIM ** -0.5)
        a_s[0, pl.ds(r0, R), :] = x[:, W:2 * W]
        a_s[1, pl.ds(r0, R), :] = x[:, 2 * W:3 * W]
        v_s[pl.ds(r0, R), :] = x[:, 3 * W:4 * W].astype(BF16)
        return carry

    lax.fori_loop(0, TL // R, prep, 0)

    o_s[...] = jnp.zeros_like(o_s)
    st_s[...] = jnp.zeros_like(st_s)
    r_i = lax.broadcasted_iota(jnp.int32, (PAIR, PAIR), 0)
    c_i = lax.broadcasted_iota(jnp.int32, (PAIR, PAIR), 1)
    eye = r_i == c_i

    def heads(x):
        return jnp.stack([x[:, h * HEAD_DIM:(h + 1) * HEAD_DIM] for h in range(HEADS)])

    def scan(s, carry):
        rows = (pl.multiple_of(s * PAIR, PAIR), pl.multiple_of(_pair_index(s, npc, npl, True) * PAIR, PAIR))
        dirs = range(2)
        lvl, e_cum, e_rest, qh, kh, vh = [], [], [], [], [], []
        for d in dirs:
            a = a_s[d, pl.ds(rows[d], PAIR), :]
            lbd = lb[d:d + 1, :]
            logf = jnp.log(lbd + (1.0 - lbd) * jax.nn.sigmoid(a))
            kh.append(heads((1.0 - lbd) * jax.nn.sigmoid(-a)))
            hi, lo = _split2(logf)
            z = _dot(pm_ref[d], hi) + _dot(pm_ref[d], lo)
            b = z[(N_LEVELS - VPU_LEVELS) * PAIR:]

            def gap_to_row(blk, r, b=b):
                ref = jnp.broadcast_to(b.reshape(PAIR // blk, blk, W)[:, r:r + 1, :], (PAIR // blk, blk, W))
                return -jnp.abs(b - ref.reshape(PAIR, W))

            ex = [jnp.exp(gap_to_row(2 * s, s - 1 if d == 0 else s)) for s in (CHUNK >> (lv + 1) for lv in range(VPU_LEVELS))]
            ex += [jnp.exp(z[j * PAIR:(j + 1) * PAIR]) for j in range(N_LEVELS - VPU_LEVELS)]
            lvl.append(ex)
            e_cum.append(heads(jnp.exp(b)))
            e_rest.append(heads(jnp.exp(gap_to_row(CHUNK, CHUNK - 1 if d == 0 else 0))))
            qh.append(heads(q_s[pl.ds(rows[d], PAIR), :]))
            vh.append(heads(v_s[pl.ds(rows[d], PAIR), :]))
        att = [jnp.where(eye, jnp.sum(qh[d] * kh[d], axis=-1, keepdims=True), 0.0) for d in dirs]
        for lv in range(N_LEVELS):
            for d in dirs:
                rr = (jnp.where(role_ref[d, lv] > 0.5, qh[d], kh[d]) * heads(lvl[d][lv])).astype(BF16)
                att[d] = att[d] + _bdot_nt(rr, rr) * msk_ref[d, lv]
        intra = [_bdot(att[d].astype(BF16), vh[d]) for d in dirs]
        qe = [(qh[d] * e_cum[d]).astype(BF16) for d in dirs]
        kd = [(kh[d] * e_rest[d]).astype(BF16) for d in dirs]
        st = [st_s[d * HEADS:(d + 1) * HEADS] for d in dirs]
        outs = [[None, None], [None, None]]
        for visit in range(2):
            for d in dirs:
                c = visit if d == 0 else 1 - visit
                cs = slice(c * CHUNK, (c + 1) * CHUNK)
                last = (c + 1) * CHUNK - 1 if d == 0 else c * CHUNK
                outs[d][c] = _bdot_nt(qe[d][:, cs], st[d].astype(BF16)) + intra[d][:, cs]
                st[d] = st[d] * e_cum[d][:, last:last + 1] + _bdot_tn(vh[d][:, cs], kd[d][:, cs])
        for d in dirs:
            st_s[d * HEADS:(d + 1) * HEADS] = st[d]
            for h in range(HEADS):
                o_s[pl.ds(rows[d], PAIR), h * HEAD_DIM:(h + 1) * HEAD_DIM] += jnp.concatenate(
                    [outs[d][0][h], outs[d][1][h]], axis=0)
        return carry

    lax.fori_loop(0, npc + npl, scan, 0)

    def gate_of(r0, blk):
        return _dot(_seq_rows(hc_ref, hl_ref, r0 // blk, blk, L), w_ref[:, 4 * W:5 * W])

    _head_norm_store(o_s, gate_of, gn_ref[...], (oc_ref, ol_ref), (L, T), R)


def _seq_rows(hc_ref, hl_ref, i, blk, L):
    assert L == blk
    r0 = pl.multiple_of(jnp.maximum(i - 1, 0) * blk, blk)
    return jnp.where(i == 0, hc_ref[...], hl_ref[pl.ds(r0, blk), :])


def _hgrn_call(hc, hl, w_c, c_lb, gn, pm, role, msk, *, layer, B, T, L):
    d = hl.shape[1]
    TL = T + L
    const = lambda shape: pl.BlockSpec(shape, lambda b: (0,) * len(shape))
    return pl.pallas_call(
        functools.partial(_hgrn_kernel, T=T, L=L, layer=layer),
        grid=(B,),
        in_specs=[
            pl.BlockSpec((L, d), lambda b: (b, 0)),
            pl.BlockSpec((T, d), lambda b: (b, 0)),
            pl.BlockSpec((None, d, 5 * MIX_HW), lambda b: (layer, 0, 0)),
            const(c_lb.shape),
            pl.BlockSpec((None, 1, HEAD_DIM), lambda b: (layer, 0, 0)),
            const(pm.shape), const(role.shape), const(msk.shape),
        ],
        out_specs=[pl.BlockSpec((L, MIX_HW), lambda b: (b, 0)), pl.BlockSpec((T, MIX_HW), lambda b: (b, 0))],
        out_shape=[jax.ShapeDtypeStruct((B * L, MIX_HW), BF16), jax.ShapeDtypeStruct((B * T, MIX_HW), BF16)],
        scratch_shapes=[
            pltpu.VMEM((TL, MIX_HW), F32),
            pltpu.VMEM((TL, MIX_HW), BF16),
            pltpu.VMEM((2, TL, MIX_HW), F32),
            pltpu.VMEM((TL, MIX_HW), F32),
            pltpu.VMEM((2 * HEADS, HEAD_DIM, HEAD_DIM), F32),
        ],
        compiler_params=_params(("parallel",)),
    )(hc, hl, w_c, c_lb, gn, pm, role, msk)


def _tri_tables():
    i = np.arange(PAIR)[:, None]
    t = np.arange(PAIR)[None, :]
    same = (i // CHUNK) == (t // CHUNK)
    return np.stack([same & (t <= i), same & (t >= i)]).astype(np.float32)


def _hgrn_tables():
    i = np.arange(PAIR)
    n_mat = N_LEVELS - VPU_LEVELS
    pm = np.zeros((2, n_mat + 1, PAIR, PAIR), np.float32)
    role = np.zeros((2, N_LEVELS, PAIR, 1), np.float32)
    msk = np.zeros((2, N_LEVELS, PAIR, PAIR), np.float32)
    same_chunk = (i[:, None] // CHUNK) == (i[None, :] // CHUNK)
    for d in range(2):
        for lv in range(N_LEVELS):
            s = CHUNK >> (lv + 1)
            mid = (i // (2 * s)) * (2 * s) + s
            is_q = (i >= mid) if d == 0 else (i < mid)
            role[d, lv, :, 0] = is_q
            for r in range(PAIR if lv >= VPU_LEVELS else 0):
                m = mid[r]
                if d == 0:
                    lo_t, hi_t = (m, r) if is_q[r] else (r + 1, m - 1)
                else:
                    lo_t, hi_t = (r, m - 1) if is_q[r] else (m, r - 1)
                pm[d, lv - VPU_LEVELS, r, lo_t:hi_t + 1] = 1.0
            same_blk = (i[:, None] // (2 * s)) == (i[None, :] // (2 * s))
            msk[d, lv] = same_blk & is_q[:, None] & ~is_q[None, :]
        t = i[None, :]
        r = i[:, None]
        pm[d, n_mat] = same_chunk & ((t <= r) if d == 0 else (t >= r))
    role = np.broadcast_to(role, (2, N_LEVELS, PAIR, HEAD_DIM)).copy()
    return pm.reshape(2, (n_mat + 1) * PAIR, PAIR), role, msk


def kernel(x, c, ctx, c_ctx, w_ada, b_ada, norm_g, w_ffn_gu, w_ffn_d, w_in, w_branch, w_out, a_qk_norm, a_sink, b_conv, b_a_log, b_dt_bias, b_norm, c_lb, c_norm):
    B, T, D = x.shape
    L = ctx.shape[1]
    depth = w_ada.shape[0]
    assert T % A_QBLK == 0 and L == 256 and D % LANES == 0

    n_rows = -(-(B + 1) // 8) * 8
    c_all = jnp.zeros((n_rows, D), F32).at[:B].set(c).at[B].set(c_ctx)
    mods = jnp.swapaxes(_ada_call(c_all, w_ada, b_ada), 1, 2)

    bf = lambda a: a.astype(BF16)
    w_gu, w_d = bf(w_ffn_gu), bf(w_ffn_d)
    o_qkv = A_Q_W + 2 * A_KV_W
    o_b = o_qkv + 4 * MIX_HW
    o_ba = o_b + 4 * HEADS
    o_c = o_ba + 5 * MIX_HW
    w_a = bf(w_in[:, :, :o_qkv])
    w_b = bf(jnp.concatenate([w_in[:, :, o_qkv:o_b],
                              jnp.pad(w_in[:, :, o_b:o_ba], ((0, 0), (0, 0), (0, LANES - 4 * HEADS)))], axis=-1))
    w_c = bf(w_in[:, :, o_ba:o_c])
    w_gate = bf(w_in[:, :, o_c:])
    w_br, w_o = bf(w_branch), bf(w_out)
    norm_g4 = norm_g.reshape(depth, 3, 1, D)

    gq = jnp.tile(a_qk_norm[:, 0:1, :], (1, 1, LANES // A_HEAD_DIM)) * (A_HEAD_DIM ** -0.5)
    gk = jnp.tile(a_qk_norm[:, 1:2, :], (1, 1, LANES // A_HEAD_DIM))
    sink = jnp.broadcast_to(a_sink[:, :, None], (depth, A_HEADS, LANES))
    pad_ba = lambda a: jnp.pad(a.reshape(depth, 1, 2 * HEADS), ((0, 0), (0, 0), (8, LANES - 8 - 2 * HEADS)))
    nalog = pad_ba(-jnp.exp(b_a_log))
    dtb = pad_ba(b_dt_bias)
    cos_t, sin_t = _rope_tables(T)
    bd = jnp.asarray(np.kron(np.eye(LANES // A_HEAD_DIM), np.ones((A_HEAD_DIM, A_HEAD_DIM))), BF16)
    tri = jnp.asarray(_tri_tables(), BF16)
    pm_np, role_np, msk_np = _hgrn_tables()
    pm, role, msk = jnp.asarray(pm_np, BF16), jnp.asarray(role_np, F32), jnp.asarray(msk_np, F32)

    xl = x.reshape(B * T, D)
    xc = ctx.reshape(B * L, D)
    tm_l, tm_c = min(1024, T), min(1024, B * L)
    row_l = lambda tm: (lambda i: (i * tm) // T)
    row_c = lambda tm: (lambda i: B)
    tf = 256

    for l in range(depth):
        last = l == depth - 1
        ffn = functools.partial(_ffn_call, mods=mods, norm_g=norm_g4, w_gu=w_gu, w_d=w_d, layer=l, tf=tf)
        xl, hl = ffn(xl, j_ffn=0, j_norm=0, mod_row=row_l(tm_l), emit_h=True, tm=tm_l)
        xc, hc = ffn(xc, j_ffn=0, j_norm=0, mod_row=row_c(tm_c), emit_h=True, tm=tm_c)
        a_c, a_l = _attn_call(hc, hl, w_a, gq, gk, sink, cos_t, sin_t, bd, layer=l, B=B, T=T, L=L)
        b_c, b_l = _delta_call(hc, hl, w_b, b_conv, nalog, dtb, b_norm.reshape(depth, 1, HEAD_DIM), tri,
                               layer=l, B=B, T=T, L=L)
        c_c, c_l = _hgrn_call(hc, hl, w_c, c_lb, c_norm.reshape(depth, 1, HEAD_DIM), pm, role, msk,
                              layer=l, B=B, T=T, L=L)
        merge = functools.partial(_merge_call, mods=mods, w_gate=w_gate, w_branch=w_br, w_out=w_o, layer=l)
        xl = merge(xl, hl, a_l, b_l, c_l, mod_row=row_l(tm_l // 2), tm=tm_l // 2)
        xl = ffn(xl, j_ffn=1, j_norm=2, mod_row=row_l(tm_l), emit_h=False, tm=tm_l)
        if not last:
            xc = merge(xc, hc, a_c, b_c, c_c, mod_row=row_c(tm_c // 2), tm=tm_c // 2)
            xc = ffn(xc, j_ffn=1, j_norm=2, mod_row=row_c(tm_c), emit_h=False, tm=tm_c)
    return xl.reshape(B, T, D)
```

```python
import functools
import math

import numpy as np
import jax
import jax.numpy as jnp
from jax import lax
from jax.experimental import pallas as pl
from jax.experimental.pallas import tpu as pltpu

F32 = jnp.float32
BF16 = jnp.bfloat16
EPS = 1e-6

GRID_W = 64
ROPE_THETA = 10000.0
N_MOD = 9

A_HEADS = 8
A_KV_HEADS = 2
A_HEAD_DIM = 64
A_WINDOW = 128
A_Q_W = A_HEADS * A_HEAD_DIM
A_KV_W = A_KV_HEADS * A_HEAD_DIM
A_QBLK = 256

HEADS = 4
HEAD_DIM = 128
MIX_HW = HEADS * HEAD_DIM
B_CONV = 5
CHUNK = 64
PAIR = 2 * CHUNK
N_LEVELS = 6
VPU_LEVELS = 4
FFN_ROWS = 512
FFN_COLS = 256
MERGE_ROWS = 512
DELTA_PREP_ROWS = 256
SOLVE_BLK = 16
DELTA_HALO = 16

LANES = 128
VMEM_LIMIT = 56 * 1024 * 1024


def _dot(a, b):
    return jnp.dot(a, b, preferred_element_type=F32)


def _dot_nt(a, b):
    return lax.dot_general(a, b, (((1,), (1,)), ((), ())), preferred_element_type=F32)


def _dot_tn(a, b):
    return lax.dot_general(a, b, (((0,), (0,)), ((), ())), preferred_element_type=F32)


def _bdot(a, b):
    return lax.dot_general(a, b, (((2,), (1,)), ((0,), (0,))), preferred_element_type=F32)


def _bdot_nt(a, b):
    return lax.dot_general(a, b, (((2,), (2,)), ((0,), (0,))), preferred_element_type=F32)


def _bdot_tn(a, b):
    return jnp.stack([_dot_tn(a[i], b[i]) for i in range(a.shape[0])])


def _split2(x):
    hi = x.astype(BF16)
    lo = (x - hi.astype(F32)).astype(BF16)
    return hi, lo


def _dot_sel(p, x):
    hi, lo = _split2(x)
    return _dot(p, hi) + _dot(p, lo)


def _silu(x):
    return x * jax.nn.sigmoid(x)


def _softplus(x):
    return jnp.maximum(x, 0.0) + jnp.log(1.0 + jnp.exp(-jnp.abs(x)))


def _rms_rows(x, g):
    return x * lax.rsqrt(jnp.mean(x * x, axis=-1, keepdims=True) + EPS) * g


def _params(sem):
    return pltpu.CompilerParams(dimension_semantics=sem, vmem_limit_bytes=VMEM_LIMIT)


def _ada_kernel(c_ref, w_ref, b_ref, o_ref):
    c = c_ref[...]
    o_ref[...] = _dot(_silu(c).astype(BF16), w_ref[...].astype(BF16)) + b_ref[...]


def _ada_call(c_all, w_ada, b_ada):
    depth, d, _ = w_ada.shape
    rows = c_all.shape[0]
    return pl.pallas_call(
        _ada_kernel,
        grid=(depth, N_MOD),
        in_specs=[
            pl.BlockSpec((rows, d), lambda l, j: (0, 0)),
            pl.BlockSpec((None, d, d), lambda l, j: (l, 0, j)),
            pl.BlockSpec((None, None, 1, d), lambda l, j: (l, j, 0, 0)),
        ],
        out_specs=pl.BlockSpec((None, None, rows, d), lambda l, j: (l, j, 0, 0)),
        out_shape=jax.ShapeDtypeStruct((depth, N_MOD, rows, d), F32),
        compiler_params=_params(("parallel", "parallel")),
    )(c_all, w_ada, b_ada.reshape(depth, N_MOD, 1, d))


def _ffn_kernel(x_ref, mod_ref, g_ref, wgu_ref, wd_ref, *rest, j_norm, emit_h, tf):
    if emit_h:
        g2_ref, o_ref, h_ref = rest
    else:
        (o_ref,) = rest
    m0 = 3 * j_norm
    d_ff = wd_ref.shape[0]
    x = x_ref[...]
    hs = (_rms_rows(x, g_ref[...]) * (1.0 + mod_ref[m0 + 1:m0 + 2, :]) + mod_ref[m0:m0 + 1, :]).astype(BF16)
    acc = None
    for c0 in range(0, d_ff, tf):
        gate = _dot(hs, wgu_ref[:, c0:c0 + tf])
        up = _dot(hs, wgu_ref[:, d_ff + c0:d_ff + c0 + tf])
        part = _dot((_silu(gate) * up).astype(BF16), wd_ref[c0:c0 + tf, :])
        acc = part if acc is None else acc + part
    out = x + 0.5 * mod_ref[m0 + 2:m0 + 3, :] * acc
    o_ref[...] = out
    if emit_h:
        y = _rms_rows(out, g2_ref[...])
        h_ref[...] = (y * (1.0 + mod_ref[4:5, :]) + mod_ref[3:4, :]).astype(BF16)


def _ffn_call(x, mods, norm_g, w_gu, w_d, *, layer, j_ffn, j_norm, mod_row, emit_h, tm, tf):
    n, d = x.shape
    f = w_d.shape[2]
    in_specs = [
        pl.BlockSpec((tm, d), lambda i: (i, 0)),
        pl.BlockSpec((None, None, N_MOD, d), lambda i: (layer, mod_row(i), 0, 0)),
        pl.BlockSpec((None, None, 1, d), lambda i: (layer, j_norm, 0, 0)),
        pl.BlockSpec((None, None, d, 2 * f), lambda i: (layer, j_ffn, 0, 0)),
        pl.BlockSpec((None, None, f, d), lambda i: (layer, j_ffn, 0, 0)),
    ]
    args = [x, mods, norm_g, w_gu, w_d]
    out_specs = [pl.BlockSpec((tm, d), lambda i: (i, 0))]
    out_shape = [jax.ShapeDtypeStruct((n, d), F32)]
    if emit_h:
        in_specs.append(pl.BlockSpec((None, None, 1, d), lambda i: (layer, 1, 0, 0)))
        args.append(norm_g)
        out_specs.append(pl.BlockSpec((tm, d), lambda i: (i, 0)))
        out_shape.append(jax.ShapeDtypeStruct((n, d), BF16))
    res = pl.pallas_call(
        functools.partial(_ffn_kernel, j_norm=j_norm, emit_h=emit_h, tf=tf),
        grid=(n // tm,),
        in_specs=in_specs,
        out_specs=out_specs,
        out_shape=out_shape,
        compiler_params=_params(("parallel",)),
    )(*args)
    return res if emit_h else res[0]


def _merge_kernel(x_ref, h_ref, oa_ref, ob_ref, oc_ref, mod_ref, wg_ref, wb_ref, wo_ref, o_ref):
    h = h_ref[...]
    d = x_ref.shape[1]
    m = None
    for i, o_r in enumerate((oa_ref, ob_ref, oc_ref)):
        gate = jax.nn.sigmoid(_dot(h, wg_ref[:, i * d:(i + 1) * d]))
        term = gate * _dot(o_r[...], wb_ref[i * MIX_HW:(i + 1) * MIX_HW, :])
        m = term if m is None else m + term
    y = _dot(m.astype(BF16), wo_ref[...])
    o_ref[...] = x_ref[...] + mod_ref[5:6, :] * y


def _merge_call(x, h, oa, ob, oc, mods, w_gate, w_branch, w_out, *, layer, mod_row, tm):
    n, d = x.shape
    tok = lambda w: pl.BlockSpec((tm, w), lambda i: (i, 0))
    return pl.pallas_call(
        _merge_kernel,
        grid=(n // tm,),
        in_specs=[
            tok(d), tok(d), tok(MIX_HW), tok(MIX_HW), tok(MIX_HW),
            pl.BlockSpec((None, None, N_MOD, d), lambda i: (layer, mod_row(i), 0, 0)),
            pl.BlockSpec((None, d, 3 * d), lambda i: (layer, 0, 0)),
            pl.BlockSpec((None, 3 * MIX_HW, d), lambda i: (layer, 0, 0)),
            pl.BlockSpec((None, d, d), lambda i: (layer, 0, 0)),
        ],
        out_specs=tok(d),
        out_shape=jax.ShapeDtypeStruct((n, d), F32),
        compiler_params=_params(("parallel",)),
    )(x, h, oa, ob, oc, mods, w_gate, w_branch, w_out)


def _group_rms(x, bd, g):
    ss = _dot_sel_rhs(x * x, bd)
    return x * lax.rsqrt(ss * (1.0 / A_HEAD_DIM) + EPS) * g


def _dot_sel_rhs(x, p):
    hi, lo = _split2(x)
    return _dot(hi, p) + _dot(lo, p)


def _rope(x, cos, sin_signed, lane):
    half = A_HEAD_DIM // 2
    partner = jnp.where((lane & (A_HEAD_DIM - 1)) < half,
                        pltpu.roll(x, LANES - half, axis=1), pltpu.roll(x, half, axis=1))
    return x * cos + partner * sin_signed


def _place(x, lane):
    sw = pltpu.roll(x, A_HEAD_DIM, axis=1)
    lo = lane < A_HEAD_DIM
    zero = jnp.zeros_like(x)
    return (jnp.where(lo, x, zero), jnp.where(lo, zero, sw),
            jnp.where(lo, sw, zero), jnp.where(lo, zero, x))


def _attn_kernel(hc_ref, hl_ref, w_ref, gq_ref, gk_ref, sink_ref, cos_ref, sin_ref, bd_ref,
                 oc_ref, ol_ref, kp_s, vp_s, kpc_s, vpc_s, *, T, L):
    bd = bd_ref[...]
    rows = A_QBLK
    lane = lax.broadcasted_iota(jnp.int32, (rows, LANES), 1)

    def kv_tiles(h_rows, cos, sin):
        kv = _dot(h_rows, w_ref[:, A_Q_W:])
        k = _group_rms(kv[:, :A_KV_W], bd, gk_ref[...])
        if cos is not None:
            k = _rope(k, cos, sin, lane)
        return _place(k, lane), _place(kv[:, A_KV_W:], lane)

    for r0 in range(0, L, rows):
        kt, vt = kv_tiles(hc_ref[r0:r0 + rows, :], None, None)
        for j in range(4):
            kpc_s[j, r0:r0 + rows, :] = kt[j].astype(BF16)
            vpc_s[j, r0:r0 + rows, :] = vt[j].astype(BF16)

    zpad = jnp.zeros((A_WINDOW, LANES), BF16)
    for j in range(4):
        kp_s[j, 0:A_WINDOW, :] = zpad
        kp_s[j, A_WINDOW + T:2 * A_WINDOW + T, :] = zpad
        vp_s[j, 0:A_WINDOW, :] = zpad
        vp_s[j, A_WINDOW + T:2 * A_WINDOW + T, :] = zpad

    def kv_body(i, carry):
        r0 = pl.multiple_of(i * rows, rows)
        kt, vt = kv_tiles(hl_ref[pl.ds(r0, rows), :], cos_ref[pl.ds(r0, rows), :], sin_ref[pl.ds(r0, rows), :])
        for j in range(4):
            kp_s[j, pl.ds(r0 + A_WINDOW, rows), :] = kt[j].astype(BF16)
            vp_s[j, pl.ds(r0 + A_WINDOW, rows), :] = vt[j].astype(BF16)
        return carry

    lax.fori_loop(0, T // rows, kv_body, 0)

    def q_tiles(h_rows, cos, sin):
        q = _dot(h_rows, w_ref[:, :A_Q_W])
        out = []
        for t in range(A_Q_W // LANES):
            qt = _group_rms(q[:, t * LANES:(t + 1) * LANES], bd, gq_ref[...])
            if cos is not None:
                qt = _rope(qt, cos, sin, lane)
            out.append(qt.astype(BF16))
        return out

    def attend(qts, k_loc, v_loc, mask_loc, o_ref, o_r0):
        def lane_fold(op, x, y=None):
            tiles = [x[:, c:c + LANES] for c in range(0, x.shape[1], LANES)]
            if y is not None:
                tiles += [y[:, c:c + LANES] for c in range(0, y.shape[1], LANES)]
            return functools.reduce(op, tiles)

        for t, qt in enumerate(qts):
            js = [2 * (t // 2) + p for p in range(2)]
            sinks = [sink_ref[2 * t + p:2 * t + p + 1, 0:1] for p in range(2)]
            s_ctx = [_dot_nt(qt, kpc_s[j]) for j in js]
            s_loc = [None, None]
            if k_loc is not None:
                s_loc = [jnp.where(mask_loc, _dot_nt(qt, k_loc(j)), -jnp.inf) for j in js]
            m = [jnp.maximum(jnp.max(lane_fold(jnp.maximum, sc, sl), axis=-1, keepdims=True), sk)
                 for sc, sl, sk in zip(s_ctx, s_loc, sinks)]
            p_ctx = [jnp.exp(sc - mi) for sc, mi in zip(s_ctx, m)]
            p_loc = [None, None]
            if k_loc is not None:
                p_loc = [jnp.exp(sl - mi) for sl, mi in zip(s_loc, m)]
            den = [jnp.sum(lane_fold(jnp.add, pc, pl_), axis=-1, keepdims=True) + jnp.exp(sk - mi)
                   for pc, pl_, sk, mi in zip(p_ctx, p_loc, sinks, m)]
            o = [_dot(pc.astype(BF16), vpc_s[j]) for pc, j in zip(p_ctx, js)]
            if k_loc is not None:
                o = [oi + _dot(pl_.astype(BF16), v_loc(j)) for oi, pl_, j in zip(o, p_loc, js)]
            acc = o[0] * (1.0 / den[0]) + o[1] * (1.0 / den[1])
            o_ref[o_r0, t * LANES:(t + 1) * LANES] = acc.astype(o_ref.dtype)

    for r0 in range(0, L, rows):
        attend(q_tiles(hc_ref[r0:r0 + rows, :], None, None), None, None, None, oc_ref, pl.ds(r0, rows))

    nwin = rows + 2 * A_WINDOW
    r_i = lax.broadcasted_iota(jnp.int32, (rows, nwin), 0)
    c_i = lax.broadcasted_iota(jnp.int32, (rows, nwin), 1)
    band = (c_i >= r_i) & (c_i <= r_i + 2 * A_WINDOW)

    def q_body(i, carry):
        r0 = pl.multiple_of(i * rows, rows)
        mask = band & (c_i >= A_WINDOW - r0) & (c_i < T + A_WINDOW - r0)
        qts = q_tiles(hl_ref[pl.ds(r0, rows), :], cos_ref[pl.ds(r0, rows), :], sin_ref[pl.ds(r0, rows), :])
        attend(qts, lambda j: kp_s[j, pl.ds(r0, nwin), :], lambda j: vp_s[j, pl.ds(r0, nwin), :],
               mask, ol_ref, pl.ds(r0, rows))
        return carry

    lax.fori_loop(0, T // rows, q_body, 0)


def _rope_tables(T):
    rows = T // GRID_W
    half = A_HEAD_DIM // 2
    row_pos = jnp.repeat(jnp.arange(rows, dtype=F32), GRID_W)
    col_pos = jnp.tile(jnp.arange(GRID_W, dtype=F32), rows)
    inv = ROPE_THETA ** (-jnp.arange(0, half, 2, dtype=F32) / half)
    ang = jnp.concatenate([row_pos[:, None] * inv, col_pos[:, None] * inv], axis=-1)
    cos, sin = jnp.cos(ang), jnp.sin(ang)
    cos_t = jnp.tile(cos, (1, LANES // half))
    sin_t = jnp.tile(jnp.concatenate([-sin, sin], axis=-1), (1, LANES // A_HEAD_DIM))
    return cos_t, sin_t


def _attn_call(hc, hl, w_a, gq, gk, sink, cos_t, sin_t, bd, *, layer, B, T, L):
    d = hl.shape[1]
    const = lambda shape: pl.BlockSpec(shape, lambda b: (0,) * len(shape))
    return pl.pallas_call(
        functools.partial(_attn_kernel, T=T, L=L),
        grid=(B,),
        in_specs=[
            pl.BlockSpec((L, d), lambda b: (b, 0)),
            pl.BlockSpec((T, d), lambda b: (b, 0)),
            pl.BlockSpec((None, d, A_Q_W + 2 * A_KV_W), lambda b: (layer, 0, 0)),
            pl.BlockSpec((None, 1, LANES), lambda b: (layer, 0, 0)),
            pl.BlockSpec((None, 1, LANES), lambda b: (layer, 0, 0)),
            pl.BlockSpec((None, A_HEADS, LANES), lambda b: (layer, 0, 0)),
            const((T, LANES)), const((T, LANES)), const((LANES, LANES)),
        ],
        out_specs=[pl.BlockSpec((L, A_Q_W), lambda b: (b, 0)), pl.BlockSpec((T, A_Q_W), lambda b: (b, 0))],
        out_shape=[jax.ShapeDtypeStruct((B * L, A_Q_W), BF16), jax.ShapeDtypeStruct((B * T, A_Q_W), BF16)],
        scratch_shapes=[
            pltpu.VMEM((4, T + 2 * A_WINDOW, LANES), BF16), pltpu.VMEM((4, T + 2 * A_WINDOW, LANES), BF16),
            pltpu.VMEM((4, L, LANES), BF16), pltpu.VMEM((4, L, LANES), BF16),
        ],
        compiler_params=_params(("parallel",)),
    )(hc, hl, w_a, gq, gk, sink, cos_t, sin_t, bd)


def _pair_index(s, npc, npl, reverse):
    if not reverse:
        return s
    return jnp.where(s < npc, npc - 1 - s, 2 * npc + npl - 1 - s)


def _head_norm_store(o_s, gate_of, gain, out_refs, seg_rows, blk):
    base = 0
    for o_ref, nrows in zip(out_refs, seg_rows):
        def body(i, carry, o_ref=o_ref, base=base):
            r0 = pl.multiple_of(i * blk, blk)
            gate = gate_of(base + r0, blk)
            for h in range(HEADS):
                sl = slice(h * HEAD_DIM, (h + 1) * HEAD_DIM)
                o = o_s[pl.ds(base + r0, blk), sl]
                y = _rms_rows(o, gain) * _silu(gate[:, sl])
                o_ref[pl.ds(r0, blk), sl] = y.astype(o_ref.dtype)
            return carry
        lax.fori_loop(0, nrows // blk, body, 0)
        base += nrows


def _delta_kernel(hc_ref, hl_ref, w_ref, cw_ref, nalog_ref, dt_ref, gn_ref, tri_ref,
                  oc_ref, ol_ref, hp_s, xc_s, qkv_s, ba_s, gt_s, o_s, st_s, *, T, L):
    TL = T + L
    QKV_W = 3 * MIX_HW
    Z0 = QKV_W
    BA0 = QKV_W + MIX_HW
    R = DELTA_PREP_ROWS
    HALO = DELTA_HALO

    zrow = jnp.zeros((HALO, hp_s.shape[1]), BF16)
    lat0 = L + 2 * HALO
    hp_s[0:HALO, :] = zrow
    hp_s[HALO:HALO + L, :] = hc_ref[...]
    hp_s[HALO + L:lat0 + HALO, :] = jnp.zeros((2 * HALO, hp_s.shape[1]), BF16)
    hp_s[lat0 + HALO:lat0 + HALO + T, :] = hl_ref[...]
    hp_s[lat0 + HALO + T:lat0 + 2 * HALO + T, :] = zrow

    lane_p = lax.broadcasted_iota(jnp.int32, (PAIR, LANES), 1)
    tri_f = tri_ref[0]
    tri_r = tri_ref[1]

    def prep(i, carry):
        r0 = pl.multiple_of(i * R, R)
        p0 = pl.multiple_of(jnp.where(i == 0, 0, r0 + 2 * HALO), HALO)
        hs = hp_s[pl.ds(p0, R + 2 * HALO), :]
        xc_s[...] = _dot(hs, w_ref[:, :QKV_W])
        y = None
        for j in range(B_CONV):
            off = HALO - B_CONV // 2 + j
            term = xc_s[off:off + R, :] * cw_ref[j:j + 1, :]
            y = term if y is None else y + term
        y = _silu(y)
        for t in range(QKV_W // LANES):
            sl = slice(t * LANES, (t + 1) * LANES)
            yt = y[:, sl]
            if t < 2 * HEADS:
                yt = yt * lax.rsqrt(jnp.sum(yt * yt, axis=-1, keepdims=True) + EPS)
                if t < HEADS:
                    yt = yt * (HEAD_DIM ** -0.5)
            qkv_s[pl.ds(r0, R), sl] = yt
        ba = _dot(hs[HALO:HALO + R, :], w_ref[:, BA0:BA0 + LANES])
        beta = jax.nn.sigmoid(ba)
        g = nalog_ref[...] * _softplus(ba + dt_ref[...])
        for u in range(R // PAIR):
            gu = g[u * PAIR:(u + 1) * PAIR, :]
            gcum = jnp.where(lane_p < 8 + HEADS, _dot_sel(tri_f, gu), _dot_sel(tri_r, gu))
            blk = jnp.where(lane_p < 8, beta[u * PAIR:(u + 1) * PAIR, :], gcum)
            ba_s[pl.ds(r0 + u * PAIR, PAIR), :] = blk
            gt_s[i * (R // PAIR) + u] = blk.T
        return carry

    lax.fori_loop(0, TL // R, prep, 0)

    o_s[...] = jnp.zeros_like(o_s)
    st_s[...] = jnp.zeros_like(st_s)

    r_i = lax.broadcasted_iota(jnp.int32, (PAIR, PAIR), 0)
    c_i = lax.broadcasted_iota(jnp.int32, (PAIR, PAIR), 1)
    same = (r_i // CHUNK) == (c_i // CHUNK)
    eye = (r_i == c_i).astype(F32)
    npc, npl = L // PAIR, T // PAIR

    dh = [(d, h) for d in range(2) for h in range(HEADS)]
    causal_m = jnp.stack([(same & ((c_i <= r_i) if d == 0 else (c_i >= r_i))).astype(F32) for d, _ in dh])
    off_diag = 1.0 - eye
    diag_blk = ((r_i // SOLVE_BLK) == (c_i // SOLVE_BLK)).astype(F32)

    def by_visit(x):
        lo, hi = x[:, :CHUNK], x[:, CHUNK:]
        return (jnp.concatenate([lo[:HEADS], hi[HEADS:]], axis=0), jnp.concatenate([hi[:HEADS], lo[HEADS:]], axis=0))

    def scan(s, carry):
        p_r = _pair_index(s, npc, npl, True)
        rows = (pl.multiple_of(s * PAIR, PAIR), pl.multiple_of(p_r * PAIR, PAIR))
        bas = (ba_s[pl.ds(rows[0], PAIR), :], ba_s[pl.ds(rows[1], PAIR), :])
        gts = (gt_s[s], gt_s[p_r])

        def grab(off):
            return jnp.stack([qkv_s[pl.ds(rows[d], PAIR), off + h * HEAD_DIM:off + (h + 1) * HEAD_DIM] for d, h in dh])

        q, k, v = grab(0), grab(MIX_HW), grab(2 * MIX_HW)
        beta = jnp.stack([bas[d][:, d * HEADS + h:d * HEADS + h + 1] for d, h in dh])
        gcol = jnp.stack([bas[d][:, 8 + d * HEADS + h:9 + d * HEADS + h] for d, h in dh])
        grow = jnp.stack([gts[d][8 + d * HEADS + h:9 + d * HEADS + h, :] for d, h in dh])
        decay = jnp.exp(jnp.where(causal_m > 0.5, gcol - grow, -jnp.inf))
        kb = k * beta
        kq =_bdot_nt(jnp.concatenate([kb, q], axis=1).astype(BF16), k.astype(BF16))
        low = kq[:, :PAIR] * decay * off_diag
        a_b = (kq[:, PAIR:] * decay).astype(BF16)
        xk = -(low * diag_blk)
        dinv = eye + xk
        for _ in range(SOLVE_BLK.bit_length() - 2):
            xb = xk.astype(BF16)
            xk = _bdot(xb, xb)
            dinv = dinv + _bdot(dinv.astype(BF16), xk.astype(BF16))
        yk = -_bdot(dinv.astype(BF16), (low * (1.0 - diag_blk)).astype(BF16))
        ninv = eye + yk
        for _ in range((CHUNK // SOLVE_BLK).bit_length() - 2):
            yb = yk.astype(BF16)
            yk = _bdot(yb, yb)
            ninv = ninv + _bdot(ninv.astype(BF16), yk.astype(BF16))
        inv = _bdot(ninv.astype(BF16), dinv.astype(BF16))
        eg = jnp.exp(gcol)
        uw = _bdot(inv.astype(BF16), jnp.concatenate([v * beta, kb * eg], axis=-1).astype(BF16))
        u12 = by_visit(uw[..., :HEAD_DIM])
        w12 = by_visit(uw[..., HEAD_DIM:].astype(BF16))
        q12 = by_visit((q * eg).astype(BF16))
        a12 = by_visit(a_b)
        k12 = by_visit(k)
        g12 = by_visit(gcol)
        glast = (jnp.concatenate([gcol[:HEADS, CHUNK - 1:CHUNK], gcol[HEADS:, CHUNK:CHUNK + 1]], axis=0),
                 jnp.concatenate([gcol[:HEADS, PAIR - 1:PAIR], gcol[HEADS:, 0:1]], axis=0))
        st = st_s[...]
        outs = []
        for c in range(2):
            stb = st.astype(BF16)
            wq_s = _bdot(jnp.concatenate([w12[c], q12[c]], axis=1), stb)
            vnb = (u12[c] - wq_s[:, :CHUNK]).astype(BF16)
            outs.append(wq_s[:, CHUNK:] + _bdot(a12[c], jnp.concatenate([vnb, vnb], axis=1)))
            kd = (k12[c] * jnp.exp(glast[c] - g12[c])).astype(BF16)
            st = st * jnp.exp(glast[c]) + _bdot_tn(kd, vnb)
        st_s[...] = st
        for i, (d, h) in enumerate(dh):
            o_pair = jnp.concatenate([outs[0][i], outs[1][i]] if d == 0 else [outs[1][i], outs[0][i]], axis=0)
            o_s[pl.ds(rows[d], PAIR), h * HEAD_DIM:(h + 1) * HEAD_DIM] += o_pair
        return carry

    lax.fori_loop(0, npc + npl, scan, 0)

    def gate_of(r0, blk):
        return _dot(_seq_rows(hc_ref, hl_ref, r0 // blk, blk, L), w_ref[:, Z0:Z0 + MIX_HW])

    _head_norm_store(o_s, gate_of, gn_ref[...], (oc_ref, ol_ref), (L, T), R)


def _delta_call(hc, hl, w_b, conv_w, nalog, dtb, gn, tri, *, layer, B, T, L):
    d = hl.shape[1]
    TL = T + L
    wcols = w_b.shape[2]
    const = lambda shape: pl.BlockSpec(shape, lambda b: (0,) * len(shape))
    return pl.pallas_call(
        functools.partial(_delta_kernel, T=T, L=L),
        grid=(B,),
        in_specs=[
            pl.BlockSpec((L, d), lambda b: (b, 0)),
            pl.BlockSpec((T, d), lambda b: (b, 0)),
            pl.BlockSpec((None, d, wcols), lambda b: (layer, 0, 0)),
            pl.BlockSpec((None, B_CONV, 3 * MIX_HW), lambda b: (layer, 0, 0)),
            pl.BlockSpec((None, 1, LANES), lambda b: (layer, 0, 0)),
            pl.BlockSpec((None, 1, LANES), lambda b: (layer, 0, 0)),
            pl.BlockSpec((None, 1, HEAD_DIM), lambda b: (layer, 0, 0)),
            const((2, PAIR, PAIR)),
        ],
        out_specs=[pl.BlockSpec((L, MIX_HW), lambda b: (b, 0)), pl.BlockSpec((T, MIX_HW), lambda b: (b, 0))],
        out_shape=[jax.ShapeDtypeStruct((B * L, MIX_HW), BF16), jax.ShapeDtypeStruct((B * T, MIX_HW), BF16)],
        scratch_shapes=[
            pltpu.VMEM((TL + 4 * DELTA_HALO, d), BF16),
            pltpu.VMEM((DELTA_PREP_ROWS + 2 * DELTA_HALO, 3 * MIX_HW), F32),
            pltpu.VMEM((TL, 3 * MIX_HW), F32),
            pltpu.VMEM((TL, LANES), F32),
            pltpu.VMEM((TL // PAIR, LANES, PAIR), F32),
            pltpu.VMEM((TL, MIX_HW), F32),
            pltpu.VMEM((2 * HEADS, HEAD_DIM, HEAD_DIM), F32),
        ],
        compiler_params=_params(("parallel",)),
    )(hc, hl, w_b, conv_w, nalog, dtb, gn, tri)


def _hgrn_kernel(hc_ref, hl_ref, w_ref, clb_ref, gn_ref, pm_ref, role_ref, msk_ref,
                 oc_ref, ol_ref, q_s, v_s, a_s, o_s, st_s, *, T, L, layer):
    TL = T + L
    R = 256
    W = MIX_HW
    npc, npl = L // PAIR, T // PAIR

    clb = clb_ref[...]
    e = jnp.exp(clb - jnp.max(clb, axis=0, keepdims=True))
    tot = jnp.sum(e, axis=0)
    part = jnp.zeros_like(tot)
    for i in range(1, layer + 1):
        part = part + e[i]
    lb = part / tot

    def prep(i, carry):
        r0 = pl.multiple_of(i * R, R)
        hs = _seq_rows(hc_ref, hl_ref, i, R, L)
        x = _dot(hs, w_ref[:, :4 * W])
        q_s[pl.ds(r0, R), :] = _silu(x[:, :W]) * (HEAD_DIM ** -0.5)
        a_s[0, pl.ds(r0, R), :] = x[:, W:2 * W]
        a_s[1, pl.ds(r0, R), :] = x[:, 2 * W:3 * W]
        v_s[pl.ds(r0, R), :] = x[:, 3 * W:4 * W].astype(BF16)
        return carry

    lax.fori_loop(0, TL // R, prep, 0)

    o_s[...] = jnp.zeros_like(o_s)
    st_s[...] = jnp.zeros_like(st_s)
    r_i = lax.broadcasted_iota(jnp.int32, (PAIR, PAIR), 0)
    c_i = lax.broadcasted_iota(jnp.int32, (PAIR, PAIR), 1)
    eye = r_i == c_i

    def heads(x):
        return jnp.stack([x[:, h * HEAD_DIM:(h + 1) * HEAD_DIM] for h in range(HEADS)])

    def scan(s, carry):
        rows = (pl.multiple_of(s * PAIR, PAIR), pl.multiple_of(_pair_index(s, npc, npl, True) * PAIR, PAIR))
        dirs = range(2)
        lvl, e_cum, e_rest, qh, kh, vh = [], [], [], [], [], []
        for d in dirs:
            a = a_s[d, pl.ds(rows[d], PAIR), :]
            lbd = lb[d:d + 1, :]
            logf = jnp.log(lbd + (1.0 - lbd) * jax.nn.sigmoid(a))
            kh.append(heads((1.0 - lbd) * jax.nn.sigmoid(-a)))
            hi, lo = _split2(logf)
            z = _dot(pm_ref[d], hi) + _dot(pm_ref[d], lo)
            b = z[(N_LEVELS - VPU_LEVELS) * PAIR:]

            def gap_to_row(blk, r, b=b):
                ref = jnp.broadcast_to(b.reshape(PAIR // blk, blk, W)[:, r:r + 1, :], (PAIR // blk, blk, W))
                return -jnp.abs(b - ref.reshape(PAIR, W))

            ex = [jnp.exp(gap_to_row(2 * s, s - 1 if d == 0 else s)) for s in (CHUNK >> (lv + 1) for lv in range(VPU_LEVELS))]
            ex += [jnp.exp(z[j * PAIR:(j + 1) * PAIR]) for j in range(N_LEVELS - VPU_LEVELS)]
            lvl.append(ex)
            e_cum.append(heads(jnp.exp(b)))
            e_rest.append(heads(jnp.exp(gap_to_row(CHUNK, CHUNK - 1 if d == 0 else 0))))
            qh.append(heads(q_s[pl.ds(rows[d], PAIR), :]))
            vh.append(heads(v_s[pl.ds(rows[d], PAIR), :]))
        att = [jnp.where(eye, jnp.sum(qh[d] * kh[d], axis=-1, keepdims=True), 0.0) for d in dirs]
        for lv in range(N_LEVELS):
            for d in dirs:
                rr = (jnp.where(role_ref[d, lv] > 0.5, qh[d], kh[d]) * heads(lvl[d][lv])).astype(BF16)
                att[d] = att[d] + _bdot_nt(rr, rr) * msk_ref[d, lv]
        intra = [_bdot(att[d].astype(BF16), vh[d]) for d in dirs]
        qe = [(qh[d] * e_cum[d]).astype(BF16) for d in dirs]
        kd = [(kh[d] * e_rest[d]).astype(BF16) for d in dirs]
        st = [st_s[d * HEADS:(d + 1) * HEADS] for d in dirs]
        outs = [[None, None], [None, None]]
        for visit in range(2):
            for d in dirs:
                c = visit if d == 0 else 1 - visit
                cs = slice(c * CHUNK, (c + 1) * CHUNK)
                last = (c + 1) * CHUNK - 1 if d == 0 else c * CHUNK
                outs[d][c] = _bdot_nt(qe[d][:, cs], st[d].astype(BF16)) + intra[d][:, cs]
                st[d] = st[d] * e_cum[d][:, last:last + 1] + _bdot_tn(vh[d][:, cs], kd[d][:, cs])
        for d in dirs:
            st_s[d * HEADS:(d + 1) * HEADS] = st[d]
            for h in range(HEADS):
                o_s[pl.ds(rows[d], PAIR), h * HEAD_DIM:(h + 1) * HEAD_DIM] += jnp.concatenate(
                    [outs[d][0][h], outs[d][1][h]], axis=0)
        return carry

    lax.fori_loop(0, npc + npl, scan, 0)

    def gate_of(r0, blk):
        return _dot(_seq_rows(hc_ref, hl_ref, r0 // blk, blk, L), w_ref[:, 4 * W:5 * W])

    _head_norm_store(o_s, gate_of, gn_ref[...], (oc_ref, ol_ref), (L, T), R)


def _seq_rows(hc_ref, hl_ref, i, blk, L):
    assert L == blk
    r0 = pl.multiple_of(jnp.maximum(i - 1, 0) * blk, blk)
    return jnp.where(i == 0, hc_ref[...], hl_ref[pl.ds(r0, blk), :])


def _hgrn_call(hc, hl, w_c, c_lb, gn, pm, role, msk, *, layer, B, T, L):
    d = hl.shape[1]
    TL = T + L
    const = lambda shape: pl.BlockSpec(shape, lambda b: (0,) * len(shape))
    return pl.pallas_call(
        functools.partial(_hgrn_kernel, T=T, L=L, layer=layer),
        grid=(B,),
        in_specs=[
            pl.BlockSpec((L, d), lambda b: (b, 0)),
            pl.BlockSpec((T, d), lambda b: (b, 0)),
            pl.BlockSpec((None, d, 5 * MIX_HW), lambda b: (layer, 0, 0)),
            const(c_lb.shape),
            pl.BlockSpec((None, 1, HEAD_DIM), lambda b: (layer, 0, 0)),
            const(pm.shape), const(role.shape), const(msk.shape),
        ],
        out_specs=[pl.BlockSpec((L, MIX_HW), lambda b: (b, 0)), pl.BlockSpec((T, MIX_HW), lambda b: (b, 0))],
        out_shape=[jax.ShapeDtypeStruct((B * L, MIX_HW), BF16), jax.ShapeDtypeStruct((B * T, MIX_HW), BF16)],
        scratch_shapes=[
            pltpu.VMEM((TL, MIX_HW), F32),
            pltpu.VMEM((TL, MIX_HW), BF16),
            pltpu.VMEM((2, TL, MIX_HW), F32),
            pltpu.VMEM((TL, MIX_HW), F32),
            pltpu.VMEM((2 * HEADS, HEAD_DIM, HEAD_DIM), F32),
        ],
        compiler_params=_params(("parallel",)),
    )(hc, hl, w_c, c_lb, gn, pm, role, msk)


def _tri_tables():
    i = np.arange(PAIR)[:, None]
    t = np.arange(PAIR)[None, :]
    same = (i // CHUNK) == (t // CHUNK)
    return np.stack([same & (t <= i), same & (t >= i)]).astype(np.float32)


def _hgrn_tables():
    i = np.arange(PAIR)
    n_mat = N_LEVELS - VPU_LEVELS
    pm = np.zeros((2, n_mat + 1, PAIR, PAIR), np.float32)
    role = np.zeros((2, N_LEVELS, PAIR, 1), np.float32)
    msk = np.zeros((2, N_LEVELS, PAIR, PAIR), np.float32)
    same_chunk = (i[:, None] // CHUNK) == (i[None, :] // CHUNK)
    for d in range(2):
        for lv in range(N_LEVELS):
            s = CHUNK >> (lv + 1)
            mid = (i // (2 * s)) * (2 * s) + s
            is_q = (i >= mid) if d == 0 else (i < mid)
            role[d, lv, :, 0] = is_q
            for r in range(PAIR if lv >= VPU_LEVELS else 0):
                m = mid[r]
                if d == 0:
                    lo_t, hi_t = (m, r) if is_q[r] else (r + 1, m - 1)
                else:
                    lo_t, hi_t = (r, m - 1) if is_q[r] else (m, r - 1)
                pm[d, lv - VPU_LEVELS, r, lo_t:hi_t + 1] = 1.0
            same_blk = (i[:, None] // (2 * s)) == (i[None, :] // (2 * s))
            msk[d, lv] = same_blk & is_q[:, None] & ~is_q[None, :]
        t = i[None, :]
        r = i[:, None]
        pm[d, n_mat] = same_chunk & ((t <= r) if d == 0 else (t >= r))
    role = np.broadcast_to(role, (2, N_LEVELS, PAIR, HEAD_DIM)).copy()
    return pm.reshape(2, (n_mat + 1) * PAIR, PAIR), role, msk


def kernel(x, c, ctx, c_ctx, w_ada, b_ada, norm_g, w_ffn_gu, w_ffn_d, w_in, w_branch, w_out, a_qk_norm, a_sink, b_conv, b_a_log, b_dt_bias, b_norm, c_lb, c_norm):
    B, T, D = x.shape
    L = ctx.shape[1]
    depth = w_ada.shape[0]
    assert T % A_QBLK == 0 and L == 256 and D % LANES == 0

    n_rows = -(-(B + 1) // 8) * 8
    c_all = jnp.zeros((n_rows, D), F32).at[:B].set(c).at[B].set(c_ctx)
    mods = jnp.swapaxes(_ada_call(c_all, w_ada, b_ada), 1, 2)

    bf = lambda a: a.astype(BF16)
    w_gu, w_d = bf(w_ffn_gu), bf(w_ffn_d)
    o_qkv = A_Q_W + 2 * A_KV_W
    o_b = o_qkv + 4 * MIX_HW
    o_ba = o_b + 4 * HEADS
    o_c = o_ba + 5 * MIX_HW
    w_a = bf(w_in[:, :, :o_qkv])
    w_b = bf(jnp.concatenate([w_in[:, :, o_qkv:o_b],
                              jnp.pad(w_in[:, :, o_b:o_ba], ((0, 0), (0, 0), (0, LANES - 4 * HEADS)))], axis=-1))
    w_c = bf(w_in[:, :, o_ba:o_c])
    w_gate = bf(w_in[:, :, o_c:])
    w_br, w_o = bf(w_branch), bf(w_out)
    norm_g4 = norm_g.reshape(depth, 3, 1, D)

    gq = jnp.tile(a_qk_norm[:, 0:1, :], (1, 1, LANES // A_HEAD_DIM)) * (A_HEAD_DIM ** -0.5)
    gk = jnp.tile(a_qk_norm[:, 1:2, :], (1, 1, LANES // A_HEAD_DIM))
    sink = jnp.broadcast_to(a_sink[:, :, None], (depth, A_HEADS, LANES))
    pad_ba = lambda a: jnp.pad(a.reshape(depth, 1, 2 * HEADS), ((0, 0), (0, 0), (8, LANES - 8 - 2 * HEADS)))
    nalog = pad_ba(-jnp.exp(b_a_log))
    dtb = pad_ba(b_dt_bias)
    cos_t, sin_t = _rope_tables(T)
    bd = jnp.asarray(np.kron(np.eye(LANES // A_HEAD_DIM), np.ones((A_HEAD_DIM, A_HEAD_DIM))), BF16)
    tri = jnp.asarray(_tri_tables(), BF16)
    pm_np, role_np, msk_np = _hgrn_tables()
    pm, role, msk = jnp.asarray(pm_np, BF16), jnp.asarray(role_np, F32), jnp.asarray(msk_np, F32)

    xl = x.reshape(B * T, D)
    xc = ctx.reshape(B * L, D)
    tm_l, tm_c = min(FFN_ROWS, T), min(FFN_ROWS, B * L)
    tm_ml, tm_mc = min(MERGE_ROWS, T), min(MERGE_ROWS, B * L)
    row_l = lambda tm: (lambda i: (i * tm) // T)
    row_c = lambda tm: (lambda i: B)
    tf = FFN_COLS

    for l in range(depth):
        last = l == depth - 1
        ffn = functools.partial(_ffn_call, mods=mods, norm_g=norm_g4, w_gu=w_gu, w_d=w_d, layer=l, tf=tf)
        xl, hl = ffn(xl, j_ffn=0, j_norm=0, mod_row=row_l(tm_l), emit_h=True, tm=tm_l)
        xc, hc = ffn(xc, j_ffn=0, j_norm=0, mod_row=row_c(tm_c), emit_h=True, tm=tm_c)
        a_c, a_l = _attn_call(hc, hl, w_a, gq, gk, sink, cos_t, sin_t, bd, layer=l, B=B, T=T, L=L)
        b_c, b_l = _delta_call(hc, hl, w_b, b_conv, nalog, dtb, b_norm.reshape(depth, 1, HEAD_DIM), tri,
                               layer=l, B=B, T=T, L=L)
        c_c, c_l = _hgrn_call(hc, hl, w_c, c_lb, c_norm.reshape(depth, 1, HEAD_DIM), pm, role, msk,
                              layer=l, B=B, T=T, L=L)
        merge = functools.partial(_merge_call, mods=mods, w_gate=w_gate, w_branch=w_br, w_out=w_o, layer=l)
        xl = merge(xl, hl, a_l, b_l, c_l, mod_row=row_l(tm_ml), tm=tm_ml)
        xl = ffn(xl, j_ffn=1, j_norm=2, mod_row=row_l(tm_l), emit_h=False, tm=tm_l)
        if not last:
            xc = merge(xc, hc, a_c, b_c, c_c, mod_row=row_c(tm_mc), tm=tm_mc)
            xc = ffn(xc, j_ffn=1, j_norm=2, mod_row=row_c(tm_c), emit_h=False, tm=tm_c)
    return xl.reshape(B, T, D)
```

```python
import functools
import math

import numpy as np
import jax
import jax.numpy as jnp
from jax import lax
from jax.experimental import pallas as pl
from jax.experimental.pallas import tpu as pltpu

F32 = jnp.float32
BF16 = jnp.bfloat16
EPS = 1e-6

GRID_W = 64
ROPE_THETA = 10000.0
N_MOD = 9

A_HEADS = 8
A_KV_HEADS = 2
A_HEAD_DIM = 64
A_WINDOW = 128
A_Q_W = A_HEADS * A_HEAD_DIM
A_KV_W = A_KV_HEADS * A_HEAD_DIM
A_QBLK = 256

HEADS = 4
HEAD_DIM = 128
MIX_HW = HEADS * HEAD_DIM
B_CONV = 5
CHUNK = 64
PAIR = 2 * CHUNK
N_LEVELS = 6
VPU_LEVELS = 4
FFN_ROWS = 512
FFN_COLS = 256
MERGE_ROWS = 512
DELTA_PREP_ROWS = 256
SOLVE_BLK = 16
DELTA_HALO = 16

LANES = 128
VMEM_LIMIT = 56 * 1024 * 1024


def _dot(a, b):
    return jnp.dot(a, b, preferred_element_type=F32)


def _dot_nt(a, b):
    return lax.dot_general(a, b, (((1,), (1,)), ((), ())), preferred_element_type=F32)


def _dot_tn(a, b):
    return lax.dot_general(a, b, (((0,), (0,)), ((), ())), preferred_element_type=F32)


def _bdot(a, b):
    return lax.dot_general(a, b, (((2,), (1,)), ((0,), (0,))), preferred_element_type=F32)


def _bdot_nt(a, b):
    return lax.dot_general(a, b, (((2,), (2,)), ((0,), (0,))), preferred_element_type=F32)


def _bdot_tn(a, b):
    return jnp.stack([_dot_tn(a[i], b[i]) for i in range(a.shape[0])])


def _split2(x):
    hi = x.astype(BF16)
    lo = (x - hi.astype(F32)).astype(BF16)
    return hi, lo


def _dot_sel(p, x):
    hi, lo = _split2(x)
    return _dot(p, hi) + _dot(p, lo)


def _silu(x):
    return x * jax.nn.sigmoid(x)


def _softplus(x):
    return jnp.maximum(x, 0.0) + jnp.log(1.0 + jnp.exp(-jnp.abs(x)))


def _rms_rows(x, g):
    return x * lax.rsqrt(jnp.mean(x * x, axis=-1, keepdims=True) + EPS) * g


def _params(sem):
    return pltpu.CompilerParams(dimension_semantics=sem, vmem_limit_bytes=VMEM_LIMIT)


def _ada_kernel(c_ref, w_ref, b_ref, o_ref):
    c = c_ref[...]
    o_ref[...] = _dot(_silu(c).astype(BF16), w_ref[...].astype(BF16)) + b_ref[...]


def _ada_call(c_all, w_ada, b_ada):
    depth, d, _ = w_ada.shape
    rows = c_all.shape[0]
    return pl.pallas_call(
        _ada_kernel,
        grid=(depth, N_MOD),
        in_specs=[
            pl.BlockSpec((rows, d), lambda l, j: (0, 0)),
            pl.BlockSpec((None, d, d), lambda l, j: (l, 0, j)),
            pl.BlockSpec((None, None, 1, d), lambda l, j: (l, j, 0, 0)),
        ],
        out_specs=pl.BlockSpec((None, None, rows, d), lambda l, j: (l, j, 0, 0)),
        out_shape=jax.ShapeDtypeStruct((depth, N_MOD, rows, d), F32),
        compiler_params=_params(("parallel", "parallel")),
    )(c_all, w_ada, b_ada.reshape(depth, N_MOD, 1, d))


def _ffn_kernel(x_ref, mod_ref, g_ref, wgu_ref, wd_ref, *rest, j_norm, emit_h, tf):
    if emit_h:
        g2_ref, o_ref, h_ref = rest
    else:
        (o_ref,) = rest
    m0 = 3 * j_norm
    d_ff = wd_ref.shape[0]
    x = x_ref[...]
    hs = (_rms_rows(x, g_ref[...]) * (1.0 + mod_ref[m0 + 1:m0 + 2, :]) + mod_ref[m0:m0 + 1, :]).astype(BF16)
    acc = None
    for c0 in range(0, d_ff, tf):
        gate = _dot(hs, wgu_ref[:, c0:c0 + tf])
        up = _dot(hs, wgu_ref[:, d_ff + c0:d_ff + c0 + tf])
        part = _dot((_silu(gate) * up).astype(BF16), wd_ref[c0:c0 + tf, :])
        acc = part if acc is None else acc + part
    out = x + 0.5 * mod_ref[m0 + 2:m0 + 3, :] * acc
    o_ref[...] = out
    if emit_h:
        y = _rms_rows(out, g2_ref[...])
        h_ref[...] = (y * (1.0 + mod_ref[4:5, :]) + mod_ref[3:4, :]).astype(BF16)


def _ffn_call(x, mods, norm_g, w_gu, w_d, *, layer, j_ffn, j_norm, mod_row, emit_h, tm, tf):
    n, d = x.shape
    f = w_d.shape[2]
    in_specs = [
        pl.BlockSpec((tm, d), lambda i: (i, 0)),
        pl.BlockSpec((None, None, N_MOD, d), lambda i: (layer, mod_row(i), 0, 0)),
        pl.BlockSpec((None, None, 1, d), lambda i: (layer, j_norm, 0, 0)),
        pl.BlockSpec((None, None, d, 2 * f), lambda i: (layer, j_ffn, 0, 0)),
        pl.BlockSpec((None, None, f, d), lambda i: (layer, j_ffn, 0, 0)),
    ]
    args = [x, mods, norm_g, w_gu, w_d]
    out_specs = [pl.BlockSpec((tm, d), lambda i: (i, 0))]
    out_shape = [jax.ShapeDtypeStruct((n, d), F32)]
    if emit_h:
        in_specs.append(pl.BlockSpec((None, None, 1, d), lambda i: (layer, 1, 0, 0)))
        args.append(norm_g)
        out_specs.append(pl.BlockSpec((tm, d), lambda i: (i, 0)))
        out_shape.append(jax.ShapeDtypeStruct((n, d), BF16))
    res = pl.pallas_call(
        functools.partial(_ffn_kernel, j_norm=j_norm, emit_h=emit_h, tf=tf),
        grid=(n // tm,),
        in_specs=in_specs,
        out_specs=out_specs,
        out_shape=out_shape,
        compiler_params=_params(("parallel",)),
    )(*args)
    return res if emit_h else res[0]


def _merge_kernel(x_ref, h_ref, oa_ref, ob_ref, oc_ref, mod_ref, wg_ref, wb_ref, wo_ref, o_ref):
    h = h_ref[...]
    d = x_ref.shape[1]
    m = None
    for i, o_r in enumerate((oa_ref, ob_ref, oc_ref)):
        gate = jax.nn.sigmoid(_dot(h, wg_ref[:, i * d:(i + 1) * d]))
        term = gate * _dot(o_r[...], wb_ref[i * MIX_HW:(i + 1) * MIX_HW, :])
        m = term if m is None else m + term
    y = _dot(m.astype(BF16), wo_ref[...])
    o_ref[...] = x_ref[...] + mod_ref[5:6, :] * y


def _merge_call(x, h, oa, ob, oc, mods, w_gate, w_branch, w_out, *, layer, mod_row, tm):
    n, d = x.shape
    tok = lambda w: pl.BlockSpec((tm, w), lambda i: (i, 0))
    return pl.pallas_call(
        _merge_kernel,
        grid=(n // tm,),
        in_specs=[
            tok(d), tok(d), tok(MIX_HW), tok(MIX_HW), tok(MIX_HW),
            pl.BlockSpec((None, None, N_MOD, d), lambda i: (layer, mod_row(i), 0, 0)),
            pl.BlockSpec((None, d, 3 * d), lambda i: (layer, 0, 0)),
            pl.BlockSpec((None, 3 * MIX_HW, d), lambda i: (layer, 0, 0)),
            pl.BlockSpec((None, d, d), lambda i: (layer, 0, 0)),
        ],
        out_specs=tok(d),
        out_shape=jax.ShapeDtypeStruct((n, d), F32),
        compiler_params=_params(("parallel",)),
    )(x, h, oa, ob, oc, mods, w_gate, w_branch, w_out)


def _group_rms(x, bd, g):
    ss = _dot_sel_rhs(x * x, bd)
    return x * lax.rsqrt(ss * (1.0 / A_HEAD_DIM) + EPS) * g


def _dot_sel_rhs(x, p):
    hi, lo = _split2(x)
    return _dot(hi, p) + _dot(lo, p)


def _rope(x, cos, sin_signed, lane):
    half = A_HEAD_DIM // 2
    partner = jnp.where((lane & (A_HEAD_DIM - 1)) < half,
                        pltpu.roll(x, LANES - half, axis=1), pltpu.roll(x, half, axis=1))
    return x * cos + partner * sin_signed


def _place(x, lane):
    sw = pltpu.roll(x, A_HEAD_DIM, axis=1)
    lo = lane < A_HEAD_DIM
    zero = jnp.zeros_like(x)
    return (jnp.where(lo, x, zero), jnp.where(lo, zero, sw),
            jnp.where(lo, sw, zero), jnp.where(lo, zero, x))


def _attn_kernel(hc_ref, hl_ref, w_ref, gq_ref, gk_ref, sink_ref, cos_ref, sin_ref, bd_ref,
                 oc_ref, ol_ref, kp_s, vp_s, kpc_s, vpc_s, *, T, L):
    bd = bd_ref[...]
    rows = A_QBLK
    lane = lax.broadcasted_iota(jnp.int32, (rows, LANES), 1)

    def kv_tiles(h_rows, cos, sin):
        kv = _dot(h_rows, w_ref[:, A_Q_W:])
        k = _group_rms(kv[:, :A_KV_W], bd, gk_ref[...])
        if cos is not None:
            k = _rope(k, cos, sin, lane)
        return _place(k, lane), _place(kv[:, A_KV_W:], lane)

    for r0 in range(0, L, rows):
        kt, vt = kv_tiles(hc_ref[r0:r0 + rows, :], None, None)
        for j in range(4):
            kpc_s[j, r0:r0 + rows, :] = kt[j].astype(BF16)
            vpc_s[j, r0:r0 + rows, :] = vt[j].astype(BF16)

    zpad = jnp.zeros((A_WINDOW, LANES), BF16)
    for j in range(4):
        kp_s[j, 0:A_WINDOW, :] = zpad
        kp_s[j, A_WINDOW + T:2 * A_WINDOW + T, :] = zpad
        vp_s[j, 0:A_WINDOW, :] = zpad
        vp_s[j, A_WINDOW + T:2 * A_WINDOW + T, :] = zpad

    def kv_body(i, carry):
        r0 = pl.multiple_of(i * rows, rows)
        kt, vt = kv_tiles(hl_ref[pl.ds(r0, rows), :], cos_ref[pl.ds(r0, rows), :], sin_ref[pl.ds(r0, rows), :])
        for j in range(4):
            kp_s[j, pl.ds(r0 + A_WINDOW, rows), :] = kt[j].astype(BF16)
            vp_s[j, pl.ds(r0 + A_WINDOW, rows), :] = vt[j].astype(BF16)
        return carry

    lax.fori_loop(0, T // rows, kv_body, 0)

    def q_tiles(h_rows, cos, sin):
        q = _dot(h_rows, w_ref[:, :A_Q_W])
        out = []
        for t in range(A_Q_W // LANES):
            qt = _group_rms(q[:, t * LANES:(t + 1) * LANES], bd, gq_ref[...])
            if cos is not None:
                qt = _rope(qt, cos, sin, lane)
            out.append(qt.astype(BF16))
        return out

    def attend(qts, k_loc, v_loc, mask_loc, o_ref, o_r0):
        def lane_fold(op, x, y=None):
            tiles = [x[:, c:c + LANES] for c in range(0, x.shape[1], LANES)]
            if y is not None:
                tiles += [y[:, c:c + LANES] for c in range(0, y.shape[1], LANES)]
            return functools.reduce(op, tiles)

        for t, qt in enumerate(qts):
            js = [2 * (t // 2) + p for p in range(2)]
            sinks = [sink_ref[2 * t + p:2 * t + p + 1, 0:1] for p in range(2)]
            s_ctx = [_dot_nt(qt, kpc_s[j]) for j in js]
            s_loc = [None, None]
            if k_loc is not None:
                s_loc = [jnp.where(mask_loc, _dot_nt(qt, k_loc(j)), -jnp.inf) for j in js]
            m = [jnp.maximum(jnp.max(lane_fold(jnp.maximum, sc, sl), axis=-1, keepdims=True), sk)
                 for sc, sl, sk in zip(s_ctx, s_loc, sinks)]
            p_ctx = [jnp.exp(sc - mi) for sc, mi in zip(s_ctx, m)]
            p_loc = [None, None]
            if k_loc is not None:
                p_loc = [jnp.exp(sl - mi) for sl, mi in zip(s_loc, m)]
            den = [jnp.sum(lane_fold(jnp.add, pc, pl_), axis=-1, keepdims=True) + jnp.exp(sk - mi)
                   for pc, pl_, sk, mi in zip(p_ctx, p_loc, sinks, m)]
            o = [_dot(pc.astype(BF16), vpc_s[j]) for pc, j in zip(p_ctx, js)]
            if k_loc is not None:
                o = [oi + _dot(pl_.astype(BF16), v_loc(j)) for oi, pl_, j in zip(o, p_loc, js)]
            acc = o[0] * (1.0 / den[0]) + o[1] * (1.0 / den[1])
            o_ref[o_r0, t * LANES:(t + 1) * LANES] = acc.astype(o_ref.dtype)

    for r0 in range(0, L, rows):
        attend(q_tiles(hc_ref[r0:r0 + rows, :], None, None), None, None, None, oc_ref, pl.ds(r0, rows))

    nwin = rows + 2 * A_WINDOW
    r_i = lax.broadcasted_iota(jnp.int32, (rows, nwin), 0)
    c_i = lax.broadcasted_iota(jnp.int32, (rows, nwin), 1)
    band = (c_i >= r_i) & (c_i <= r_i + 2 * A_WINDOW)

    def q_body(i, carry):
        r0 = pl.multiple_of(i * rows, rows)
        mask = band & (c_i >= A_WINDOW - r0) & (c_i < T + A_WINDOW - r0)
        qts = q_tiles(hl_ref[pl.ds(r0, rows), :], cos_ref[pl.ds(r0, rows), :], sin_ref[pl.ds(r0, rows), :])
        attend(qts, lambda j: kp_s[j, pl.ds(r0, nwin), :], lambda j: vp_s[j, pl.ds(r0, nwin), :],
               mask, ol_ref, pl.ds(r0, rows))
        return carry

    lax.fori_loop(0, T // rows, q_body, 0)


def _rope_tables(T):
    rows = T // GRID_W
    half = A_HEAD_DIM // 2
    row_pos = jnp.repeat(jnp.arange(rows, dtype=F32), GRID_W)
    col_pos = jnp.tile(jnp.arange(GRID_W, dtype=F32), rows)
    inv = ROPE_THETA ** (-jnp.arange(0, half, 2, dtype=F32) / half)
    ang = jnp.concatenate([row_pos[:, None] * inv, col_pos[:, None] * inv], axis=-1)
    cos, sin = jnp.cos(ang), jnp.sin(ang)
    cos_t = jnp.tile(cos, (1, LANES // half))
    sin_t = jnp.tile(jnp.concatenate([-sin, sin], axis=-1), (1, LANES // A_HEAD_DIM))
    return cos_t, sin_t


def _attn_call(hc, hl, w_a, gq, gk, sink, cos_t, sin_t, bd, *, layer, B, T, L):
    d = hl.shape[1]
    const = lambda shape: pl.BlockSpec(shape, lambda b: (0,) * len(shape))
    return pl.pallas_call(
        functools.partial(_attn_kernel, T=T, L=L),
        grid=(B,),
        in_specs=[
            pl.BlockSpec((L, d), lambda b: (b, 0)),
            pl.BlockSpec((T, d), lambda b: (b, 0)),
            pl.BlockSpec((None, d, A_Q_W + 2 * A_KV_W), lambda b: (layer, 0, 0)),
            pl.BlockSpec((None, 1, LANES), lambda b: (layer, 0, 0)),
            pl.BlockSpec((None, 1, LANES), lambda b: (layer, 0, 0)),
            pl.BlockSpec((None, A_HEADS, LANES), lambda b: (layer, 0, 0)),
            const((T, LANES)), const((T, LANES)), const((LANES, LANES)),
        ],
        out_specs=[pl.BlockSpec((L, A_Q_W), lambda b: (b, 0)), pl.BlockSpec((T, A_Q_W), lambda b: (b, 0))],
        out_shape=[jax.ShapeDtypeStruct((B * L, A_Q_W), BF16), jax.ShapeDtypeStruct((B * T, A_Q_W), BF16)],
        scratch_shapes=[
            pltpu.VMEM((4, T + 2 * A_WINDOW, LANES), BF16), pltpu.VMEM((4, T + 2 * A_WINDOW, LANES), BF16),
            pltpu.VMEM((4, L, LANES), BF16), pltpu.VMEM((4, L, LANES), BF16),
        ],
        compiler_params=_params(("parallel",)),
    )(hc, hl, w_a, gq, gk, sink, cos_t, sin_t, bd)


def _pair_index(s, npc, npl, reverse):
    if not reverse:
        return s
    return jnp.where(s < npc, npc - 1 - s, 2 * npc + npl - 1 - s)


def _head_norm_store(o_s, gate_of, gain, out_refs, seg_rows, blk):
    base = 0
    for o_ref, nrows in zip(out_refs, seg_rows):
        def body(i, carry, o_ref=o_ref, base=base):
            r0 = pl.multiple_of(i * blk, blk)
            gate = gate_of(base + r0, blk)
            for h in range(HEADS):
                sl = slice(h * HEAD_DIM, (h + 1) * HEAD_DIM)
                o = o_s[pl.ds(base + r0, blk), sl]
                y = _rms_rows(o, gain) * _silu(gate[:, sl])
                o_ref[pl.ds(r0, blk), sl] = y.astype(o_ref.dtype)
            return carry
        lax.fori_loop(0, nrows // blk, body, 0)
        base += nrows


def _delta_kernel(hc_ref, hl_ref, w_ref, cw_ref, nalog_ref, dt_ref, gn_ref, tri_ref,
                  oc_ref, ol_ref, hp_s, xc_s, qkv_s, ba_s, gt_s, o_s, st_s, *, T, L):
    TL = T + L
    QKV_W = 3 * MIX_HW
    Z0 = QKV_W
    BA0 = QKV_W + MIX_HW
    R = DELTA_PREP_ROWS
    HALO = DELTA_HALO

    zrow = jnp.zeros((HALO, hp_s.shape[1]), BF16)
    lat0 = L + 2 * HALO
    hp_s[0:HALO, :] = zrow
    hp_s[HALO:HALO + L, :] = hc_ref[...]
    hp_s[HALO + L:lat0 + HALO, :] = jnp.zeros((2 * HALO, hp_s.shape[1]), BF16)
    hp_s[lat0 + HALO:lat0 + HALO + T, :] = hl_ref[...]
    hp_s[lat0 + HALO + T:lat0 + 2 * HALO + T, :] = zrow

    lane_p = lax.broadcasted_iota(jnp.int32, (PAIR, LANES), 1)
    tri_f = tri_ref[0]
    tri_r = tri_ref[1]

    def prep(i, carry):
        r0 = pl.multiple_of(i * R, R)
        p0 = pl.multiple_of(jnp.where(i == 0, 0, r0 + 2 * HALO), HALO)
        hs = hp_s[pl.ds(p0, R + 2 * HALO), :]
        xc_s[...] = _dot(hs, w_ref[:, :QKV_W])
        y = None
        for j in range(B_CONV):
            off = HALO - B_CONV // 2 + j
            term = xc_s[off:off + R, :] * cw_ref[j:j + 1, :]
            y = term if y is None else y + term
        y = _silu(y)
        for t in range(QKV_W // LANES):
            sl = slice(t * LANES, (t + 1) * LANES)
            yt = y[:, sl]
            if t < 2 * HEADS:
                yt = yt * lax.rsqrt(jnp.sum(yt * yt, axis=-1, keepdims=True) + EPS)
                if t < HEADS:
                    yt = yt * (HEAD_DIM ** -0.5)
            qkv_s[pl.ds(r0, R), sl] = yt
        ba = _dot(hs[HALO:HALO + R, :], w_ref[:, BA0:BA0 + LANES])
        beta = jax.nn.sigmoid(ba)
        g = nalog_ref[...] * _softplus(ba + dt_ref[...])
        for u in range(R // PAIR):
            gu = g[u * PAIR:(u + 1) * PAIR, :]
            gcum = jnp.where(lane_p < 8 + HEADS, _dot_sel(tri_f, gu), _dot_sel(tri_r, gu))
            blk = jnp.where(lane_p < 8, beta[u * PAIR:(u + 1) * PAIR, :], gcum)
            ba_s[pl.ds(r0 + u * PAIR, PAIR), :] = blk
            gt_s[i * (R // PAIR) + u] = blk.T
        return carry

    lax.fori_loop(0, TL // R, prep, 0)

    o_s[...] = jnp.zeros_like(o_s)
    st_s[...] = jnp.zeros_like(st_s)

    r_i = lax.broadcasted_iota(jnp.int32, (PAIR, PAIR), 0)
    c_i = lax.broadcasted_iota(jnp.int32, (PAIR, PAIR), 1)
    same = (r_i // CHUNK) == (c_i // CHUNK)
    eye = (r_i == c_i).astype(F32)
    npc, npl = L // PAIR, T // PAIR

    dh = [(d, h) for d in range(2) for h in range(HEADS)]
    causal_m = jnp.stack([(same & ((c_i <= r_i) if d == 0 else (c_i >= r_i))).astype(F32) for d, _ in dh])
    off_diag = 1.0 - eye
    diag_blk = ((r_i // SOLVE_BLK) == (c_i // SOLVE_BLK)).astype(F32)

    def by_visit(x):
        lo, hi = x[:, :CHUNK], x[:, CHUNK:]
        return (jnp.concatenate([lo[:HEADS], hi[HEADS:]], axis=0), jnp.concatenate([hi[:HEADS], lo[HEADS:]], axis=0))

    NB = 2 * HEADS
    causal_m2 = jnp.concatenate([causal_m, causal_m], axis=0)

    def scan(t, carry):
        steps = (2 * t, 2 * t + 1)
        pairs = [(s, _pair_index(s, npc, npl, True)) for s in steps]
        rows_all = [(pl.multiple_of(pf * PAIR, PAIR), pl.multiple_of(pr * PAIR, PAIR)) for pf, pr in pairs]
        sdh = [(i, d, h) for i in range(2) for d, h in dh]
        bas = [[ba_s[pl.ds(r, PAIR), :] for r in rows] for rows in rows_all]
        gts = [[gt_s[p] for p in pp] for pp in pairs]

        def grab(off):
            return jnp.stack([qkv_s[pl.ds(rows_all[i][d], PAIR), off + h * HEAD_DIM:off + (h + 1) * HEAD_DIM]
                              for i, d, h in sdh])

        q, k, v = grab(0), grab(MIX_HW), grab(2 * MIX_HW)
        beta = jnp.stack([bas[i][d][:, d * HEADS + h:d * HEADS + h + 1] for i, d, h in sdh])
        gcol_all = jnp.stack([bas[i][d][:, 8 + d * HEADS + h:9 + d * HEADS + h] for i, d, h in sdh])
        grow = jnp.stack([gts[i][d][8 + d * HEADS + h:9 + d * HEADS + h, :] for i, d, h in sdh])
        decay = jnp.exp(jnp.where(causal_m2 > 0.5, gcol_all - grow, -jnp.inf))
        kb = k * beta
        kq =_bdot_nt(jnp.concatenate([kb, q], axis=1).astype(BF16), k.astype(BF16))
        low = kq[:, :PAIR] * decay * off_diag
        a_b = (kq[:, PAIR:] * decay).astype(BF16)
        xk = -(low * diag_blk)
        dinv = eye + xk
        for _ in range(SOLVE_BLK.bit_length() - 2):
            xb = xk.astype(BF16)
            xk = _bdot(xb, xb)
            dinv = dinv + _bdot(dinv.astype(BF16), xk.astype(BF16))
        yk = -_bdot(dinv.astype(BF16), (low * (1.0 - diag_blk)).astype(BF16))
        ninv = eye + yk
        for _ in range((CHUNK // SOLVE_BLK).bit_length() - 2):
            yb = yk.astype(BF16)
            yk = _bdot(yb, yb)
            ninv = ninv + _bdot(ninv.astype(BF16), yk.astype(BF16))
        inv = _bdot(ninv.astype(BF16), dinv.astype(BF16))
        eg = jnp.exp(gcol_all)
        uw = _bdot(inv.astype(BF16), jnp.concatenate([v * beta, kb * eg], axis=-1).astype(BF16))
        qg = (q * eg).astype(BF16)
        st = st_s[...]
        for i in range(2):
            one = lambda x, i=i: x[i * NB:(i + 1) * NB]
            gcol = one(gcol_all)
            u12 = by_visit(one(uw)[..., :HEAD_DIM])
            w12 = by_visit(one(uw)[..., HEAD_DIM:].astype(BF16))
            q12 = by_visit(one(qg))
            a12 = by_visit(one(a_b))
            k12 = by_visit(one(k))
            g12 = by_visit(gcol)
            glast = (jnp.concatenate([gcol[:HEADS, CHUNK - 1:CHUNK], gcol[HEADS:, CHUNK:CHUNK + 1]], axis=0),
                     jnp.concatenate([gcol[:HEADS, PAIR - 1:PAIR], gcol[HEADS:, 0:1]], axis=0))
            outs = []
            for c in range(2):
                stb = st.astype(BF16)
                wq_s = _bdot(jnp.concatenate([w12[c], q12[c]], axis=1), stb)
                vnb = (u12[c] - wq_s[:, :CHUNK]).astype(BF16)
                outs.append(wq_s[:, CHUNK:] + _bdot(a12[c], jnp.concatenate([vnb, vnb], axis=1)))
                kd = (k12[c] * jnp.exp(glast[c] - g12[c])).astype(BF16)
                st = st * jnp.exp(glast[c]) + _bdot_tn(kd, vnb)
            for j, (d, h) in enumerate(dh):
                o_pair = jnp.concatenate([outs[0][j], outs[1][j]] if d == 0 else [outs[1][j], outs[0][j]], axis=0)
                o_s[pl.ds(rows_all[i][d], PAIR), h * HEAD_DIM:(h + 1) * HEAD_DIM] += o_pair
        st_s[...] = st
        return carry

    assert (npc + npl) % 2 == 0
    lax.fori_loop(0, (npc + npl) // 2, scan, 0)

    def gate_of(r0, blk):
        return _dot(_seq_rows(hc_ref, hl_ref, r0 // blk, blk, L), w_ref[:, Z0:Z0 + MIX_HW])

    _head_norm_store(o_s, gate_of, gn_ref[...], (oc_ref, ol_ref), (L, T), R)


def _delta_call(hc, hl, w_b, conv_w, nalog, dtb, gn, tri, *, layer, B, T, L):
    d = hl.shape[1]
    TL = T + L
    wcols = w_b.shape[2]
    const = lambda shape: pl.BlockSpec(shape, lambda b: (0,) * len(shape))
    return pl.pallas_call(
        functools.partial(_delta_kernel, T=T, L=L),
        grid=(B,),
        in_specs=[
            pl.BlockSpec((L, d), lambda b: (b, 0)),
            pl.BlockSpec((T, d), lambda b: (b, 0)),
            pl.BlockSpec((None, d, wcols), lambda b: (layer, 0, 0)),
            pl.BlockSpec((None, B_CONV, 3 * MIX_HW), lambda b: (layer, 0, 0)),
            pl.BlockSpec((None, 1, LANES), lambda b: (layer, 0, 0)),
            pl.BlockSpec((None, 1, LANES), lambda b: (layer, 0, 0)),
            pl.BlockSpec((None, 1, HEAD_DIM), lambda b: (layer, 0, 0)),
            const((2, PAIR, PAIR)),
        ],
        out_specs=[pl.BlockSpec((L, MIX_HW), lambda b: (b, 0)), pl.BlockSpec((T, MIX_HW), lambda b: (b, 0))],
        out_shape=[jax.ShapeDtypeStruct((B * L, MIX_HW), BF16), jax.ShapeDtypeStruct((B * T, MIX_HW), BF16)],
        scratch_shapes=[
            pltpu.VMEM((TL + 4 * DELTA_HALO, d), BF16),
            pltpu.VMEM((DELTA_PREP_ROWS + 2 * DELTA_HALO, 3 * MIX_HW), F32),
            pltpu.VMEM((TL, 3 * MIX_HW), F32),
            pltpu.VMEM((TL, LANES), F32),
            pltpu.VMEM((TL // PAIR, LANES, PAIR), F32),
            pltpu.VMEM((TL, MIX_HW), F32),
            pltpu.VMEM((2 * HEADS, HEAD_DIM, HEAD_DIM), F32),
        ],
        compiler_params=_params(("parallel",)),
    )(hc, hl, w_b, conv_w, nalog, dtb, gn, tri)


def _hgrn_kernel(hc_ref, hl_ref, w_ref, clb_ref, gn_ref, pm_ref, role_ref, msk_ref,
                 oc_ref, ol_ref, q_s, v_s, a_s, o_s, st_s, *, T, L, layer):
    TL = T + L
    R = 256
    W = MIX_HW
    npc, npl = L // PAIR, T // PAIR

    clb = clb_ref[...]
    e = jnp.exp(clb - jnp.max(clb, axis=0, keepdims=True))
    tot = jnp.sum(e, axis=0)
    part = jnp.zeros_like(tot)
    for i in range(1, layer + 1):
        part = part + e[i]
    lb = part / tot

    def prep(i, carry):
        r0 = pl.multiple_of(i * R, R)
        hs = _seq_rows(hc_ref, hl_ref, i, R, L)
        x = _dot(hs, w_ref[:, :4 * W])
        q_s[pl.ds(r0, R), :] = _silu(x[:, :W]) * (HEAD_DIM ** -0.5)
        a_s[0, pl.ds(r0, R), :] = x[:, W:2 * W]
        a_s[1, pl.ds(r0, R), :] = x[:, 2 * W:3 * W]
        v_s[pl.ds(r0, R), :] = x[:, 3 * W:4 * W].astype(BF16)
        return carry

    lax.fori_loop(0, TL // R, prep, 0)

    o_s[...] = jnp.zeros_like(o_s)
    st_s[...] = jnp.zeros_like(st_s)
    r_i = lax.broadcasted_iota(jnp.int32, (PAIR, PAIR), 0)
    c_i = lax.broadcasted_iota(jnp.int32, (PAIR, PAIR), 1)
    eye = r_i == c_i

    def heads(x):
        return jnp.stack([x[:, h * HEAD_DIM:(h + 1) * HEAD_DIM] for h in range(HEADS)])

    def scan(s, carry):
        rows = (pl.multiple_of(s * PAIR, PAIR), pl.multiple_of(_pair_index(s, npc, npl, True) * PAIR, PAIR))
        dirs = range(2)
        lvl, e_cum, e_rest, qh, kh, vh = [], [], [], [], [], []
        for d in dirs:
            a = a_s[d, pl.ds(rows[d], PAIR), :]
            lbd = lb[d:d + 1, :]
            logf = jnp.log(lbd + (1.0 - lbd) * jax.nn.sigmoid(a))
            kh.append(heads((1.0 - lbd) * jax.nn.sigmoid(-a)))
            hi, lo = _split2(logf)
            z = _dot(pm_ref[d], hi) + _dot(pm_ref[d], lo)
            b = z[(N_LEVELS - VPU_LEVELS) * PAIR:]

            def gap_to_row(blk, r, b=b):
                ref = jnp.broadcast_to(b.reshape(PAIR // blk, blk, W)[:, r:r + 1, :], (PAIR // blk, blk, W))
                return -jnp.abs(b - ref.reshape(PAIR, W))

            ex = [jnp.exp(gap_to_row(2 * s, s - 1 if d == 0 else s)) for s in (CHUNK >> (lv + 1) for lv in range(VPU_LEVELS))]
            ex += [jnp.exp(z[j * PAIR:(j + 1) * PAIR]) for j in range(N_LEVELS - VPU_LEVELS)]
            lvl.append(ex)
            e_cum.append(heads(jnp.exp(b)))
            e_rest.append(heads(jnp.exp(gap_to_row(CHUNK, CHUNK - 1 if d == 0 else 0))))
            qh.append(heads(q_s[pl.ds(rows[d], PAIR), :]))
            vh.append(heads(v_s[pl.ds(rows[d], PAIR), :]))
        att = [jnp.where(eye, jnp.sum(qh[d] * kh[d], axis=-1, keepdims=True), 0.0) for d in dirs]
        for lv in range(N_LEVELS):
            for d in dirs:
                rr = (jnp.where(role_ref[d, lv] > 0.5, qh[d], kh[d]) * heads(lvl[d][lv])).astype(BF16)
                att[d] = att[d] + _bdot_nt(rr, rr) * msk_ref[d, lv]
        intra = [_bdot(att[d].astype(BF16), vh[d]) for d in dirs]
        qe = [(qh[d] * e_cum[d]).astype(BF16) for d in dirs]
        kd = [(kh[d] * e_rest[d]).astype(BF16) for d in dirs]
        st = [st_s[d * HEADS:(d + 1) * HEADS] for d in dirs]
        outs = [[None, None], [None, None]]
        for visit in range(2):
            for d in dirs:
                c = visit if d == 0 else 1 - visit
                cs = slice(c * CHUNK, (c + 1) * CHUNK)
                last = (c + 1) * CHUNK - 1 if d == 0 else c * CHUNK
                outs[d][c] = _bdot_nt(qe[d][:, cs], st[d].astype(BF16)) + intra[d][:, cs]
                st[d] = st[d] * e_cum[d][:, last:last + 1] + _bdot_tn(vh[d][:, cs], kd[d][:, cs])
        for d in dirs:
            st_s[d * HEADS:(d + 1) * HEADS] = st[d]
            for h in range(HEADS):
                o_s[pl.ds(rows[d], PAIR), h * HEAD_DIM:(h + 1) * HEAD_DIM] += jnp.concatenate(
                    [outs[d][0][h], outs[d][1][h]], axis=0)
        return carry

    lax.fori_loop(0, npc + npl, scan, 0)

    def gate_of(r0, blk):
        return _dot(_seq_rows(hc_ref, hl_ref, r0 // blk, blk, L), w_ref[:, 4 * W:5 * W])

    _head_norm_store(o_s, gate_of, gn_ref[...], (oc_ref, ol_ref), (L, T), R)


def _seq_rows(hc_ref, hl_ref, i, blk, L):
    assert L == blk
    r0 = pl.multiple_of(jnp.maximum(i - 1, 0) * blk, blk)
    return jnp.where(i == 0, hc_ref[...], hl_ref[pl.ds(r0, blk), :])


def _hgrn_call(hc, hl, w_c, c_lb, gn, pm, role, msk, *, layer, B, T, L):
    d = hl.shape[1]
    TL = T + L
    const = lambda shape: pl.BlockSpec(shape, lambda b: (0,) * len(shape))
    return pl.pallas_call(
        functools.partial(_hgrn_kernel, T=T, L=L, layer=layer),
        grid=(B,),
        in_specs=[
            pl.BlockSpec((L, d), lambda b: (b, 0)),
            pl.BlockSpec((T, d), lambda b: (b, 0)),
            pl.BlockSpec((None, d, 5 * MIX_HW), lambda b: (layer, 0, 0)),
            const(c_lb.shape),
            pl.BlockSpec((None, 1, HEAD_DIM), lambda b: (layer, 0, 0)),
            const(pm.shape), const(role.shape), const(msk.shape),
        ],
        out_specs=[pl.BlockSpec((L, MIX_HW), lambda b: (b, 0)), pl.BlockSpec((T, MIX_HW), lambda b: (b, 0))],
        out_shape=[jax.ShapeDtypeStruct((B * L, MIX_HW), BF16), jax.ShapeDtypeStruct((B * T, MIX_HW), BF16)],
        scratch_shapes=[
            pltpu.VMEM((TL, MIX_HW), F32),
            pltpu.VMEM((TL, MIX_HW), BF16),
            pltpu.VMEM((2, TL, MIX_HW), F32),
            pltpu.VMEM((TL, MIX_HW), F32),
            pltpu.VMEM((2 * HEADS, HEAD_DIM, HEAD_DIM), F32),
        ],
        compiler_params=_params(("parallel",)),
    )(hc, hl, w_c, c_lb, gn, pm, role, msk)


def _tri_tables():
    i = np.arange(PAIR)[:, None]
    t = np.arange(PAIR)[None, :]
    same = (i // CHUNK) == (t // CHUNK)
    return np.stack([same & (t <= i), same & (t >= i)]).astype(np.float32)


def _hgrn_tables():
    i = np.arange(PAIR)
    n_mat = N_LEVELS - VPU_LEVELS
    pm = np.zeros((2, n_mat + 1, PAIR, PAIR), np.float32)
    role = np.zeros((2, N_LEVELS, PAIR, 1), np.float32)
    msk = np.zeros((2, N_LEVELS, PAIR, PAIR), np.float32)
    same_chunk = (i[:, None] // CHUNK) == (i[None, :] // CHUNK)
    for d in range(2):
        for lv in range(N_LEVELS):
            s = CHUNK >> (lv + 1)
            mid = (i // (2 * s)) * (2 * s) + s
            is_q = (i >= mid) if d == 0 else (i < mid)
            role[d, lv, :, 0] = is_q
            for r in range(PAIR if lv >= VPU_LEVELS else 0):
                m = mid[r]
                if d == 0:
                    lo_t, hi_t = (m, r) if is_q[r] else (r + 1, m - 1)
                else:
                    lo_t, hi_t = (r, m - 1) if is_q[r] else (m, r - 1)
                pm[d, lv - VPU_LEVELS, r, lo_t:hi_t + 1] = 1.0
            same_blk = (i[:, None] // (2 * s)) == (i[None, :] // (2 * s))
            msk[d, lv] = same_blk & is_q[:, None] & ~is_q[None, :]
        t = i[None, :]
        r = i[:, None]
        pm[d, n_mat] = same_chunk & ((t <= r) if d == 0 else (t >= r))
    role = np.broadcast_to(role, (2, N_LEVELS, PAIR, HEAD_DIM)).copy()
    return pm.reshape(2, (n_mat + 1) * PAIR, PAIR), role, msk


def kernel(x, c, ctx, c_ctx, w_ada, b_ada, norm_g, w_ffn_gu, w_ffn_d, w_in, w_branch, w_out, a_qk_norm, a_sink, b_conv, b_a_log, b_dt_bias, b_norm, c_lb, c_norm):
    B, T, D = x.shape
    L = ctx.shape[1]
    depth = w_ada.shape[0]
    assert T % A_QBLK == 0 and L == 256 and D % LANES == 0

    n_rows = -(-(B + 1) // 8) * 8
    c_all = jnp.zeros((n_rows, D), F32).at[:B].set(c).at[B].set(c_ctx)
    mods = jnp.swapaxes(_ada_call(c_all, w_ada, b_ada), 1, 2)

    bf = lambda a: a.astype(BF16)
    w_gu, w_d = bf(w_ffn_gu), bf(w_ffn_d)
    o_qkv = A_Q_W + 2 * A_KV_W
    o_b = o_qkv + 4 * MIX_HW
    o_ba = o_b + 4 * HEADS
    o_c = o_ba + 5 * MIX_HW
    w_a = bf(w_in[:, :, :o_qkv])
    w_b = bf(jnp.concatenate([w_in[:, :, o_qkv:o_b],
                              jnp.pad(w_in[:, :, o_b:o_ba], ((0, 0), (0, 0), (0, LANES - 4 * HEADS)))], axis=-1))
    w_c = bf(w_in[:, :, o_ba:o_c])
    w_gate = bf(w_in[:, :, o_c:])
    w_br, w_o = bf(w_branch), bf(w_out)
    norm_g4 = norm_g.reshape(depth, 3, 1, D)

    gq = jnp.tile(a_qk_norm[:, 0:1, :], (1, 1, LANES // A_HEAD_DIM)) * (A_HEAD_DIM ** -0.5)
    gk = jnp.tile(a_qk_norm[:, 1:2, :], (1, 1, LANES // A_HEAD_DIM))
    sink = jnp.broadcast_to(a_sink[:, :, None], (depth, A_HEADS, LANES))
    pad_ba = lambda a: jnp.pad(a.reshape(depth, 1, 2 * HEADS), ((0, 0), (0, 0), (8, LANES - 8 - 2 * HEADS)))
    nalog = pad_ba(-jnp.exp(b_a_log))
    dtb = pad_ba(b_dt_bias)
    cos_t, sin_t = _rope_tables(T)
    bd = jnp.asarray(np.kron(np.eye(LANES // A_HEAD_DIM), np.ones((A_HEAD_DIM, A_HEAD_DIM))), BF16)
    tri = jnp.asarray(_tri_tables(), BF16)
    pm_np, role_np, msk_np = _hgrn_tables()
    pm, role, msk = jnp.asarray(pm_np, BF16), jnp.asarray(role_np, F32), jnp.asarray(msk_np, F32)

    xl = x.reshape(B * T, D)
    xc = ctx.reshape(B * L, D)
    tm_l, tm_c = min(FFN_ROWS, T), min(FFN_ROWS, B * L)
    tm_ml, tm_mc = min(MERGE_ROWS, T), min(MERGE_ROWS, B * L)
    row_l = lambda tm: (lambda i: (i * tm) // T)
    row_c = lambda tm: (lambda i: B)
    tf = FFN_COLS

    for l in range(depth):
        last = l == depth - 1
        ffn = functools.partial(_ffn_call, mods=mods, norm_g=norm_g4, w_gu=w_gu, w_d=w_d, layer=l, tf=tf)
        xl, hl = ffn(xl, j_ffn=0, j_norm=0, mod_row=row_l(tm_l), emit_h=True, tm=tm_l)
        xc, hc = ffn(xc, j_ffn=0, j_norm=0, mod_row=row_c(tm_c), emit_h=True, tm=tm_c)
        a_c, a_l = _attn_call(hc, hl, w_a, gq, gk, sink, cos_t, sin_t, bd, layer=l, B=B, T=T, L=L)
        b_c, b_l = _delta_call(hc, hl, w_b, b_conv, nalog, dtb, b_norm.reshape(depth, 1, HEAD_DIM), tri,
                               layer=l, B=B, T=T, L=L)
        c_c, c_l = _hgrn_call(hc, hl, w_c, c_lb, c_norm.reshape(depth, 1, HEAD_DIM), pm, role, msk,
                              layer=l, B=B, T=T, L=L)
        merge = functools.partial(_merge_call, mods=mods, w_gate=w_gate, w_branch=w_br, w_out=w_o, layer=l)
        xl = merge(xl, hl, a_l, b_l, c_l, mod_row=row_l(tm_ml), tm=tm_ml)
        xl = ffn(xl, j_ffn=1, j_norm=2, mod_row=row_l(tm_l), emit_h=False, tm=tm_l)
        if not last:
            xc = merge(xc, hc, a_c, b_c, c_c, mod_row=row_c(tm_mc), tm=tm_mc)
            xc = ffn(xc, j_ffn=1, j_norm=2, mod_row=row_c(tm_c), emit_h=False, tm=tm_c)
    return xl.reshape(B, T, D)
```

```python
import functools
import math

import numpy as np
import jax
import jax.numpy as jnp
from jax import lax
from jax.experimental import pallas as pl
from jax.experimental.pallas import tpu as pltpu

F32 = jnp.float32
BF16 = jnp.bfloat16
EPS = 1e-6

GRID_W = 64
ROPE_THETA = 10000.0
N_MOD = 9

A_HEADS = 8
A_KV_HEADS = 2
A_HEAD_DIM = 64
A_WINDOW = 128
A_Q_W = A_HEADS * A_HEAD_DIM
A_KV_W = A_KV_HEADS * A_HEAD_DIM
A_QBLK = 256

HEADS = 4
HEAD_DIM = 128
MIX_HW = HEADS * HEAD_DIM
B_CONV = 5
CHUNK = 64
PAIR = 2 * CHUNK
N_LEVELS = 6
VPU_LEVELS = 4
FFN_ROWS = 1024
FFN_COLS = 256
MERGE_ROWS = 512
DELTA_PREP_ROWS = 256
SOLVE_BLK = 16
DELTA_HALO = 16

LANES = 128
VMEM_LIMIT = 56 * 1024 * 1024


def _dot(a, b):
    return jnp.dot(a, b, preferred_element_type=F32)


def _dot_nt(a, b):
    return lax.dot_general(a, b, (((1,), (1,)), ((), ())), preferred_element_type=F32)


def _dot_tn(a, b):
    return lax.dot_general(a, b, (((0,), (0,)), ((), ())), preferred_element_type=F32)


def _bdot(a, b):
    return lax.dot_general(a, b, (((2,), (1,)), ((0,), (0,))), preferred_element_type=F32)


def _bdot_nt(a, b):
    return lax.dot_general(a, b, (((2,), (2,)), ((0,), (0,))), preferred_element_type=F32)


def _bdot_tn(a, b):
    return jnp.stack([_dot_tn(a[i], b[i]) for i in range(a.shape[0])])


def _split2(x):
    hi = x.astype(BF16)
    lo = (x - hi.astype(F32)).astype(BF16)
    return hi, lo


def _dot_sel(p, x):
    hi, lo = _split2(x)
    return _dot(p, hi) + _dot(p, lo)


def _silu(x):
    return x * jax.nn.sigmoid(x)


def _softplus(x):
    return jnp.maximum(x, 0.0) + jnp.log(1.0 + jnp.exp(-jnp.abs(x)))


def _rms_rows(x, g):
    return x * lax.rsqrt(jnp.mean(x * x, axis=-1, keepdims=True) + EPS) * g


def _params(sem):
    return pltpu.CompilerParams(dimension_semantics=sem, vmem_limit_bytes=VMEM_LIMIT)


def _ada_kernel(c_ref, w_ref, b_ref, o_ref):
    c = c_ref[...]
    o_ref[...] = _dot(_silu(c).astype(BF16), w_ref[...].astype(BF16)) + b_ref[...]


def _ada_call(c_all, w_ada, b_ada):
    depth, d, _ = w_ada.shape
    rows = c_all.shape[0]
    return pl.pallas_call(
        _ada_kernel,
        grid=(depth, N_MOD),
        in_specs=[
            pl.BlockSpec((rows, d), lambda l, j: (0, 0)),
            pl.BlockSpec((None, d, d), lambda l, j: (l, 0, j)),
            pl.BlockSpec((None, None, 1, d), lambda l, j: (l, j, 0, 0)),
        ],
        out_specs=pl.BlockSpec((None, None, rows, d), lambda l, j: (l, j, 0, 0)),
        out_shape=jax.ShapeDtypeStruct((depth, N_MOD, rows, d), F32),
        compiler_params=_params(("parallel", "parallel")),
    )(c_all, w_ada, b_ada.reshape(depth, N_MOD, 1, d))


def _ffn_kernel(x_ref, mod_ref, g_ref, wgu_ref, wd_ref, *rest, j_norm, emit_h, tf):
    if emit_h:
        g2_ref, o_ref, h_ref = rest
    else:
        (o_ref,) = rest
    m0 = 3 * j_norm
    d_ff = wd_ref.shape[0]
    x = x_ref[...]
    hs = (_rms_rows(x, g_ref[...]) * (1.0 + mod_ref[m0 + 1:m0 + 2, :]) + mod_ref[m0:m0 + 1, :]).astype(BF16)
    acc = None
    for c0 in range(0, d_ff, tf):
        gate = _dot(hs, wgu_ref[:, c0:c0 + tf])
        up = _dot(hs, wgu_ref[:, d_ff + c0:d_ff + c0 + tf])
        part = _dot((_silu(gate) * up).astype(BF16), wd_ref[c0:c0 + tf, :])
        acc = part if acc is None else acc + part
    out = x + 0.5 * mod_ref[m0 + 2:m0 + 3, :] * acc
    o_ref[...] = out
    if emit_h:
        y = _rms_rows(out, g2_ref[...])
        h_ref[...] = (y * (1.0 + mod_ref[4:5, :]) + mod_ref[3:4, :]).astype(BF16)


def _ffn_call(x, mods, norm_g, w_gu, w_d, *, layer, j_ffn, j_norm, mod_row, emit_h, tm, tf):
    n, d = x.shape
    f = w_d.shape[2]
    in_specs = [
        pl.BlockSpec((tm, d), lambda i: (i, 0)),
        pl.BlockSpec((None, None, N_MOD, d), lambda i: (layer, mod_row(i), 0, 0)),
        pl.BlockSpec((None, None, 1, d), lambda i: (layer, j_norm, 0, 0)),
        pl.BlockSpec((None, None, d, 2 * f), lambda i: (layer, j_ffn, 0, 0)),
        pl.BlockSpec((None, None, f, d), lambda i: (layer, j_ffn, 0, 0)),
    ]
    args = [x, mods, norm_g, w_gu, w_d]
    out_specs = [pl.BlockSpec((tm, d), lambda i: (i, 0))]
    out_shape = [jax.ShapeDtypeStruct((n, d), F32)]
    if emit_h:
        in_specs.append(pl.BlockSpec((None, None, 1, d), lambda i: (layer, 1, 0, 0)))
        args.append(norm_g)
        out_specs.append(pl.BlockSpec((tm, d), lambda i: (i, 0)))
        out_shape.append(jax.ShapeDtypeStruct((n, d), BF16))
    res = pl.pallas_call(
        functools.partial(_ffn_kernel, j_norm=j_norm, emit_h=emit_h, tf=tf),
        grid=(n // tm,),
        in_specs=in_specs,
        out_specs=out_specs,
        out_shape=out_shape,
        compiler_params=_params(("parallel",)),
    )(*args)
    return res if emit_h else res[0]


def _merge_kernel(x_ref, h_ref, oa_ref, ob_ref, oc_ref, mod_ref, wg_ref, wb_ref, wo_ref, o_ref):
    h = h_ref[...]
    d = x_ref.shape[1]
    m = None
    for i, o_r in enumerate((oa_ref, ob_ref, oc_ref)):
        gate = jax.nn.sigmoid(_dot(h, wg_ref[:, i * d:(i + 1) * d]))
        term = gate * _dot(o_r[...], wb_ref[i * MIX_HW:(i + 1) * MIX_HW, :])
        m = term if m is None else m + term
    y = _dot(m.astype(BF16), wo_ref[...])
    o_ref[...] = x_ref[...] + mod_ref[5:6, :] * y


def _merge_call(x, h, oa, ob, oc, mods, w_gate, w_branch, w_out, *, layer, mod_row, tm):
    n, d = x.shape
    tok = lambda w: pl.BlockSpec((tm, w), lambda i: (i, 0))
    return pl.pallas_call(
        _merge_kernel,
        grid=(n // tm,),
        in_specs=[
            tok(d), tok(d), tok(MIX_HW), tok(MIX_HW), tok(MIX_HW),
            pl.BlockSpec((None, None, N_MOD, d), lambda i: (layer, mod_row(i), 0, 0)),
            pl.BlockSpec((None, d, 3 * d), lambda i: (layer, 0, 0)),
            pl.BlockSpec((None, 3 * MIX_HW, d), lambda i: (layer, 0, 0)),
            pl.BlockSpec((None, d, d), lambda i: (layer, 0, 0)),
        ],
        out_specs=tok(d),
        out_shape=jax.ShapeDtypeStruct((n, d), F32),
        compiler_params=_params(("parallel",)),
    )(x, h, oa, ob, oc, mods, w_gate, w_branch, w_out)


def _group_rms(x, bd, g):
    ss = _dot_sel_rhs(x * x, bd)
    return x * lax.rsqrt(ss * (1.0 / A_HEAD_DIM) + EPS) * g


def _dot_sel_rhs(x, p):
    hi, lo = _split2(x)
    return _dot(hi, p) + _dot(lo, p)


def _rope(x, cos, sin_signed, lane):
    half = A_HEAD_DIM // 2
    partner = jnp.where((lane & (A_HEAD_DIM - 1)) < half,
                        pltpu.roll(x, LANES - half, axis=1), pltpu.roll(x, half, axis=1))
    return x * cos + partner * sin_signed


def _place(x, lane):
    sw = pltpu.roll(x, A_HEAD_DIM, axis=1)
    lo = lane < A_HEAD_DIM
    zero = jnp.zeros_like(x)
    return (jnp.where(lo, x, zero), jnp.where(lo, zero, sw),
            jnp.where(lo, sw, zero), jnp.where(lo, zero, x))


def _attn_kernel(hc_ref, hl_ref, w_ref, gq_ref, gk_ref, sink_ref, cos_ref, sin_ref, bd_ref,
                 oc_ref, ol_ref, kp_s, vp_s, kpc_s, vpc_s, *, T, L):
    bd = bd_ref[...]
    rows = A_QBLK
    lane = lax.broadcasted_iota(jnp.int32, (rows, LANES), 1)

    def kv_tiles(h_rows, cos, sin):
        kv = _dot(h_rows, w_ref[:, A_Q_W:])
        k = _group_rms(kv[:, :A_KV_W], bd, gk_ref[...])
        if cos is not None:
            k = _rope(k, cos, sin, lane)
        return _place(k, lane), _place(kv[:, A_KV_W:], lane)

    for r0 in range(0, L, rows):
        kt, vt = kv_tiles(hc_ref[r0:r0 + rows, :], None, None)
        for j in range(4):
            kpc_s[j, r0:r0 + rows, :] = kt[j].astype(BF16)
            vpc_s[j, r0:r0 + rows, :] = vt[j].astype(BF16)

    zpad = jnp.zeros((A_WINDOW, LANES), BF16)
    for j in range(4):
        kp_s[j, 0:A_WINDOW, :] = zpad
        kp_s[j, A_WINDOW + T:2 * A_WINDOW + T, :] = zpad
        vp_s[j, 0:A_WINDOW, :] = zpad
        vp_s[j, A_WINDOW + T:2 * A_WINDOW + T, :] = zpad

    def kv_body(i, carry):
        r0 = pl.multiple_of(i * rows, rows)
        kt, vt = kv_tiles(hl_ref[pl.ds(r0, rows), :], cos_ref[pl.ds(r0, rows), :], sin_ref[pl.ds(r0, rows), :])
        for j in range(4):
            kp_s[j, pl.ds(r0 + A_WINDOW, rows), :] = kt[j].astype(BF16)
            vp_s[j, pl.ds(r0 + A_WINDOW, rows), :] = vt[j].astype(BF16)
        return carry

    lax.fori_loop(0, T // rows, kv_body, 0)

    def q_tiles(h_rows, cos, sin):
        q = _dot(h_rows, w_ref[:, :A_Q_W])
        out = []
        for t in range(A_Q_W // LANES):
            qt = _group_rms(q[:, t * LANES:(t + 1) * LANES], bd, gq_ref[...])
            if cos is not None:
                qt = _rope(qt, cos, sin, lane)
            out.append(qt.astype(BF16))
        return out

    def attend(qts, k_loc, v_loc, mask_loc, o_ref, o_r0):
        def lane_fold(op, x, y=None):
            tiles = [x[:, c:c + LANES] for c in range(0, x.shape[1], LANES)]
            if y is not None:
                tiles += [y[:, c:c + LANES] for c in range(0, y.shape[1], LANES)]
            return functools.reduce(op, tiles)

        for t, qt in enumerate(qts):
            js = [2 * (t // 2) + p for p in range(2)]
            sinks = [sink_ref[2 * t + p:2 * t + p + 1, 0:1] for p in range(2)]
            s_ctx = [_dot_nt(qt, kpc_s[j]) for j in js]
            s_loc = [None, None]
            if k_loc is not None:
                s_loc = [jnp.where(mask_loc, _dot_nt(qt, k_loc(j)), -jnp.inf) for j in js]
            m = [jnp.maximum(jnp.max(lane_fold(jnp.maximum, sc, sl), axis=-1, keepdims=True), sk)
                 for sc, sl, sk in zip(s_ctx, s_loc, sinks)]
            p_ctx = [jnp.exp(sc - mi) for sc, mi in zip(s_ctx, m)]
            p_loc = [None, None]
            if k_loc is not None:
                p_loc = [jnp.exp(sl - mi) for sl, mi in zip(s_loc, m)]
            den = [jnp.sum(lane_fold(jnp.add, pc, pl_), axis=-1, keepdims=True) + jnp.exp(sk - mi)
                   for pc, pl_, sk, mi in zip(p_ctx, p_loc, sinks, m)]
            o = [_dot(pc.astype(BF16), vpc_s[j]) for pc, j in zip(p_ctx, js)]
            if k_loc is not None:
                o = [oi + _dot(pl_.astype(BF16), v_loc(j)) for oi, pl_, j in zip(o, p_loc, js)]
            acc = o[0] * (1.0 / den[0]) + o[1] * (1.0 / den[1])
            o_ref[o_r0, t * LANES:(t + 1) * LANES] = acc.astype(o_ref.dtype)

    for r0 in range(0, L, rows):
        attend(q_tiles(hc_ref[r0:r0 + rows, :], None, None), None, None, None, oc_ref, pl.ds(r0, rows))

    nwin = rows + 2 * A_WINDOW
    r_i = lax.broadcasted_iota(jnp.int32, (rows, nwin), 0)
    c_i = lax.broadcasted_iota(jnp.int32, (rows, nwin), 1)
    band = (c_i >= r_i) & (c_i <= r_i + 2 * A_WINDOW)

    def q_body(i, carry):
        r0 = pl.multiple_of(i * rows, rows)
        mask = band & (c_i >= A_WINDOW - r0) & (c_i < T + A_WINDOW - r0)
        qts = q_tiles(hl_ref[pl.ds(r0, rows), :], cos_ref[pl.ds(r0, rows), :], sin_ref[pl.ds(r0, rows), :])
        attend(qts, lambda j: kp_s[j, pl.ds(r0, nwin), :], lambda j: vp_s[j, pl.ds(r0, nwin), :],
               mask, ol_ref, pl.ds(r0, rows))
        return carry

    lax.fori_loop(0, T // rows, q_body, 0)


def _rope_tables(T):
    rows = T // GRID_W
    half = A_HEAD_DIM // 2
    row_pos = jnp.repeat(jnp.arange(rows, dtype=F32), GRID_W)
    col_pos = jnp.tile(jnp.arange(GRID_W, dtype=F32), rows)
    inv = ROPE_THETA ** (-jnp.arange(0, half, 2, dtype=F32) / half)
    ang = jnp.concatenate([row_pos[:, None] * inv, col_pos[:, None] * inv], axis=-1)
    cos, sin = jnp.cos(ang), jnp.sin(ang)
    cos_t = jnp.tile(cos, (1, LANES // half))
    sin_t = jnp.tile(jnp.concatenate([-sin, sin], axis=-1), (1, LANES // A_HEAD_DIM))
    return cos_t, sin_t


def _attn_call(hc, hl, w_a, gq, gk, sink, cos_t, sin_t, bd, *, layer, B, T, L):
    d = hl.shape[1]
    const = lambda shape: pl.BlockSpec(shape, lambda b: (0,) * len(shape))
    return pl.pallas_call(
        functools.partial(_attn_kernel, T=T, L=L),
        grid=(B,),
        in_specs=[
            pl.BlockSpec((L, d), lambda b: (b, 0)),
            pl.BlockSpec((T, d), lambda b: (b, 0)),
            pl.BlockSpec((None, d, A_Q_W + 2 * A_KV_W), lambda b: (layer, 0, 0)),
            pl.BlockSpec((None, 1, LANES), lambda b: (layer, 0, 0)),
            pl.BlockSpec((None, 1, LANES), lambda b: (layer, 0, 0)),
            pl.BlockSpec((None, A_HEADS, LANES), lambda b: (layer, 0, 0)),
            const((T, LANES)), const((T, LANES)), const((LANES, LANES)),
        ],
        out_specs=[pl.BlockSpec((L, A_Q_W), lambda b: (b, 0)), pl.BlockSpec((T, A_Q_W), lambda b: (b, 0))],
        out_shape=[jax.ShapeDtypeStruct((B * L, A_Q_W), BF16), jax.ShapeDtypeStruct((B * T, A_Q_W), BF16)],
        scratch_shapes=[
            pltpu.VMEM((4, T + 2 * A_WINDOW, LANES), BF16), pltpu.VMEM((4, T + 2 * A_WINDOW, LANES), BF16),
            pltpu.VMEM((4, L, LANES), BF16), pltpu.VMEM((4, L, LANES), BF16),
        ],
        compiler_params=_params(("parallel",)),
    )(hc, hl, w_a, gq, gk, sink, cos_t, sin_t, bd)


def _pair_index(s, npc, npl, reverse):
    if not reverse:
        return s
    return jnp.where(s < npc, npc - 1 - s, 2 * npc + npl - 1 - s)


def _head_norm_store(o_s, gate_of, gain, out_refs, seg_rows, blk):
    base = 0
    for o_ref, nrows in zip(out_refs, seg_rows):
        def body(i, carry, o_ref=o_ref, base=base):
            r0 = pl.multiple_of(i * blk, blk)
            gate = gate_of(base + r0, blk)
            for h in range(HEADS):
                sl = slice(h * HEAD_DIM, (h + 1) * HEAD_DIM)
                o = o_s[pl.ds(base + r0, blk), sl]
                y = _rms_rows(o, gain) * _silu(gate[:, sl])
                o_ref[pl.ds(r0, blk), sl] = y.astype(o_ref.dtype)
            return carry
        lax.fori_loop(0, nrows // blk, body, 0)
        base += nrows


def _delta_kernel(hc_ref, hl_ref, w_ref, cw_ref, nalog_ref, dt_ref, gn_ref, tri_ref,
                  oc_ref, ol_ref, hp_s, xc_s, qkv_s, ba_s, gt_s, o_s, st_s, *, T, L):
    TL = T + L
    QKV_W = 3 * MIX_HW
    Z0 = QKV_W
    BA0 = QKV_W + MIX_HW
    R = DELTA_PREP_ROWS
    HALO = DELTA_HALO

    zrow = jnp.zeros((HALO, hp_s.shape[1]), BF16)
    lat0 = L + 2 * HALO
    hp_s[0:HALO, :] = zrow
    hp_s[HALO:HALO + L, :] = hc_ref[...]
    hp_s[HALO + L:lat0 + HALO, :] = jnp.zeros((2 * HALO, hp_s.shape[1]), BF16)
    hp_s[lat0 + HALO:lat0 + HALO + T, :] = hl_ref[...]
    hp_s[lat0 + HALO + T:lat0 + 2 * HALO + T, :] = zrow

    lane_p = lax.broadcasted_iota(jnp.int32, (PAIR, LANES), 1)
    tri_f = tri_ref[0]
    tri_r = tri_ref[1]

    def prep(i, carry):
        r0 = pl.multiple_of(i * R, R)
        p0 = pl.multiple_of(jnp.where(i == 0, 0, r0 + 2 * HALO), HALO)
        hs = hp_s[pl.ds(p0, R + 2 * HALO), :]
        xc_s[...] = _dot(hs, w_ref[:, :QKV_W])
        y = None
        for j in range(B_CONV):
            off = HALO - B_CONV // 2 + j
            term = xc_s[off:off + R, :] * cw_ref[j:j + 1, :]
            y = term if y is None else y + term
        y = _silu(y)
        for t in range(QKV_W // LANES):
            sl = slice(t * LANES, (t + 1) * LANES)
            yt = y[:, sl]
            if t < 2 * HEADS:
                yt = yt * lax.rsqrt(jnp.sum(yt * yt, axis=-1, keepdims=True) + EPS)
                if t < HEADS:
                    yt = yt * (HEAD_DIM ** -0.5)
            qkv_s[pl.ds(r0, R), sl] = yt
        ba = _dot(hs[HALO:HALO + R, :], w_ref[:, BA0:BA0 + LANES])
        beta = jax.nn.sigmoid(ba)
        g = nalog_ref[...] * _softplus(ba + dt_ref[...])
        for u in range(R // PAIR):
            gu = g[u * PAIR:(u + 1) * PAIR, :]
            gcum = jnp.where(lane_p < 8 + HEADS, _dot_sel(tri_f, gu), _dot_sel(tri_r, gu))
            blk = jnp.where(lane_p < 8, beta[u * PAIR:(u + 1) * PAIR, :], gcum)
            ba_s[pl.ds(r0 + u * PAIR, PAIR), :] = blk
            gt_s[i * (R // PAIR) + u] = blk.T
        return carry

    lax.fori_loop(0, TL // R, prep, 0)

    o_s[...] = jnp.zeros_like(o_s)
    st_s[...] = jnp.zeros_like(st_s)

    r_i = lax.broadcasted_iota(jnp.int32, (PAIR, PAIR), 0)
    c_i = lax.broadcasted_iota(jnp.int32, (PAIR, PAIR), 1)
    same = (r_i // CHUNK) == (c_i // CHUNK)
    eye = (r_i == c_i).astype(F32)
    npc, npl = L // PAIR, T // PAIR

    dh = [(d, h) for d in range(2) for h in range(HEADS)]
    causal_m = jnp.stack([(same & ((c_i <= r_i) if d == 0 else (c_i >= r_i))).astype(F32) for d, _ in dh])
    off_diag = 1.0 - eye
    diag_blk = ((r_i // SOLVE_BLK) == (c_i // SOLVE_BLK)).astype(F32)

    def by_visit(x):
        lo, hi = x[:, :CHUNK], x[:, CHUNK:]
        return (jnp.concatenate([lo[:HEADS], hi[HEADS:]], axis=0), jnp.concatenate([hi[:HEADS], lo[HEADS:]], axis=0))

    NB = 2 * HEADS
    causal_m2 = jnp.concatenate([causal_m, causal_m], axis=0)

    def scan(t, carry):
        steps = (2 * t, 2 * t + 1)
        pairs = [(s, _pair_index(s, npc, npl, True)) for s in steps]
        rows_all = [(pl.multiple_of(pf * PAIR, PAIR), pl.multiple_of(pr * PAIR, PAIR)) for pf, pr in pairs]
        sdh = [(i, d, h) for i in range(2) for d, h in dh]
        bas = [[ba_s[pl.ds(r, PAIR), :] for r in rows] for rows in rows_all]
        gts = [[gt_s[p] for p in pp] for pp in pairs]

        def grab(off):
            return jnp.stack([qkv_s[pl.ds(rows_all[i][d], PAIR), off + h * HEAD_DIM:off + (h + 1) * HEAD_DIM]
                              for i, d, h in sdh])

        q, k, v = grab(0), grab(MIX_HW), grab(2 * MIX_HW)
        beta = jnp.stack([bas[i][d][:, d * HEADS + h:d * HEADS + h + 1] for i, d, h in sdh])
        gcol_all = jnp.stack([bas[i][d][:, 8 + d * HEADS + h:9 + d * HEADS + h] for i, d, h in sdh])
        grow = jnp.stack([gts[i][d][8 + d * HEADS + h:9 + d * HEADS + h, :] for i, d, h in sdh])
        decay = jnp.exp(jnp.where(causal_m2 > 0.5, gcol_all - grow, -jnp.inf))
        kb = k * beta
        kq =_bdot_nt(jnp.concatenate([kb, q], axis=1).astype(BF16), k.astype(BF16))
        low = kq[:, :PAIR] * decay * off_diag
        a_b = (kq[:, PAIR:] * decay).astype(BF16)
        xk = -(low * diag_blk)
        dinv = eye + xk
        for _ in range(SOLVE_BLK.bit_length() - 2):
            xb = xk.astype(BF16)
            xk = _bdot(xb, xb)
            dinv = dinv + _bdot(dinv.astype(BF16), xk.astype(BF16))
        yk = -_bdot(dinv.astype(BF16), (low * (1.0 - diag_blk)).astype(BF16))
        ninv = eye + yk
        for _ in range((CHUNK // SOLVE_BLK).bit_length() - 2):
            yb = yk.astype(BF16)
            yk = _bdot(yb, yb)
            ninv = ninv + _bdot(ninv.astype(BF16), yk.astype(BF16))
        inv = _bdot(ninv.astype(BF16), dinv.astype(BF16))
        eg = jnp.exp(gcol_all)
        uw = _bdot(inv.astype(BF16), jnp.concatenate([v * beta, kb * eg], axis=-1).astype(BF16))
        qg = (q * eg).astype(BF16)
        st = st_s[...]
        for i in range(2):
            one = lambda x, i=i: x[i * NB:(i + 1) * NB]
            gcol = one(gcol_all)
            u12 = by_visit(one(uw)[..., :HEAD_DIM])
            w12 = by_visit(one(uw)[..., HEAD_DIM:].astype(BF16))
            q12 = by_visit(one(qg))
            a12 = by_visit(one(a_b))
            k12 = by_visit(one(k))
            g12 = by_visit(gcol)
            glast = (jnp.concatenate([gcol[:HEADS, CHUNK - 1:CHUNK], gcol[HEADS:, CHUNK:CHUNK + 1]], axis=0),
                     jnp.concatenate([gcol[:HEADS, PAIR - 1:PAIR], gcol[HEADS:, 0:1]], axis=0))
            outs = []
            for c in range(2):
                stb = st.astype(BF16)
                wq_s = _bdot(jnp.concatenate([w12[c], q12[c]], axis=1), stb)
                vnb = (u12[c] - wq_s[:, :CHUNK]).astype(BF16)
                outs.append(wq_s[:, CHUNK:] + _bdot(a12[c], jnp.concatenate([vnb, vnb], axis=1)))
                kd = (k12[c] * jnp.exp(glast[c] - g12[c])).astype(BF16)
                st = st * jnp.exp(glast[c]) + _bdot_tn(kd, vnb)
            for j, (d, h) in enumerate(dh):
                o_pair = jnp.concatenate([outs[0][j], outs[1][j]] if d == 0 else [outs[1][j], outs[0][j]], axis=0)
                o_s[pl.ds(rows_all[i][d], PAIR), h * HEAD_DIM:(h + 1) * HEAD_DIM] += o_pair
        st_s[...] = st
        return carry

    assert (npc + npl) % 2 == 0
    lax.fori_loop(0, (npc + npl) // 2, scan, 0)

    def gate_of(r0, blk):
        return _dot(_seq_rows(hc_ref, hl_ref, r0 // blk, blk, L), w_ref[:, Z0:Z0 + MIX_HW])

    _head_norm_store(o_s, gate_of, gn_ref[...], (oc_ref, ol_ref), (L, T), R)


def _delta_call(hc, hl, w_b, conv_w, nalog, dtb, gn, tri, *, layer, B, T, L):
    d = hl.shape[1]
    TL = T + L
    wcols = w_b.shape[2]
    const = lambda shape: pl.BlockSpec(shape, lambda b: (0,) * len(shape))
    return pl.pallas_call(
        functools.partial(_delta_kernel, T=T, L=L),
        grid=(B,),
        in_specs=[
            pl.BlockSpec((L, d), lambda b: (b, 0)),
            pl.BlockSpec((T, d), lambda b: (b, 0)),
            pl.BlockSpec((None, d, wcols), lambda b: (layer, 0, 0)),
            pl.BlockSpec((None, B_CONV, 3 * MIX_HW), lambda b: (layer, 0, 0)),
            pl.BlockSpec((None, 1, LANES), lambda b: (layer, 0, 0)),
            pl.BlockSpec((None, 1, LANES), lambda b: (layer, 0, 0)),
            pl.BlockSpec((None, 1, HEAD_DIM), lambda b: (layer, 0, 0)),
            const((2, PAIR, PAIR)),
        ],
        out_specs=[pl.BlockSpec((L, MIX_HW), lambda b: (b, 0)), pl.BlockSpec((T, MIX_HW), lambda b: (b, 0))],
        out_shape=[jax.ShapeDtypeStruct((B * L, MIX_HW), BF16), jax.ShapeDtypeStruct((B * T, MIX_HW), BF16)],
        scratch_shapes=[
            pltpu.VMEM((TL + 4 * DELTA_HALO, d), BF16),
            pltpu.VMEM((DELTA_PREP_ROWS + 2 * DELTA_HALO, 3 * MIX_HW), F32),
            pltpu.VMEM((TL, 3 * MIX_HW), F32),
            pltpu.VMEM((TL, LANES), F32),
            pltpu.VMEM((TL // PAIR, LANES, PAIR), F32),
            pltpu.VMEM((TL, MIX_HW), F32),
            pltpu.VMEM((2 * HEADS, HEAD_DIM, HEAD_DIM), F32),
        ],
        compiler_params=_params(("parallel",)),
    )(hc, hl, w_b, conv_w, nalog, dtb, gn, tri)


def _hgrn_kernel(hc_ref, hl_ref, w_ref, clb_ref, gn_ref, pm_ref, role_ref, msk_ref,
                 oc_ref, ol_ref, q_s, v_s, a_s, o_s, st_s, *, T, L, layer):
    TL = T + L
    R = 256
    W = MIX_HW
    npc, npl = L // PAIR, T // PAIR

    clb = clb_ref[...]
    e = jnp.exp(clb - jnp.max(clb, axis=0, keepdims=True))
    tot = jnp.sum(e, axis=0)
    part = jnp.zeros_like(tot)
    for i in range(1, layer + 1):
        part = part + e[i]
    lb = part / tot

    def prep(i, carry):
        r0 = pl.multiple_of(i * R, R)
        hs = _seq_rows(hc_ref, hl_ref, i, R, L)
        x = _dot(hs, w_ref[:, :4 * W])
        q_s[pl.ds(r0, R), :] = _silu(x[:, :W]) * (HEAD_DIM ** -0.5)
        a_s[0, pl.ds(r0, R), :] = x[:, W:2 * W]
        a_s[1, pl.ds(r0, R), :] = x[:, 2 * W:3 * W]
        v_s[pl.ds(r0, R), :] = x[:, 3 * W:4 * W].astype(BF16)
        return carry

    lax.fori_loop(0, TL // R, prep, 0)

    o_s[...] = jnp.zeros_like(o_s)
    st_s[...] = jnp.zeros_like(st_s)
    r_i = lax.broadcasted_iota(jnp.int32, (PAIR, PAIR), 0)
    c_i = lax.broadcasted_iota(jnp.int32, (PAIR, PAIR), 1)
    eye = r_i == c_i

    def heads(x):
        return jnp.stack([x[:, h * HEAD_DIM:(h + 1) * HEAD_DIM] for h in range(HEADS)])

    def scan(s, carry):
        rows = (pl.multiple_of(s * PAIR, PAIR), pl.multiple_of(_pair_index(s, npc, npl, True) * PAIR, PAIR))
        dirs = range(2)
        lvl, e_cum, e_rest, qh, kh, vh = [], [], [], [], [], []
        for d in dirs:
            a = a_s[d, pl.ds(rows[d], PAIR), :]
            lbd = lb[d:d + 1, :]
            logf = jnp.log(lbd + (1.0 - lbd) * jax.nn.sigmoid(a))
            kh.append(heads((1.0 - lbd) * jax.nn.sigmoid(-a)))
            hi, lo = _split2(logf)
            z = _dot(pm_ref[d], hi) + _dot(pm_ref[d], lo)
            b = z[(N_LEVELS - VPU_LEVELS) * PAIR:]

            def gap_to_row(blk, r, b=b):
                ref = jnp.broadcast_to(b.reshape(PAIR // blk, blk, W)[:, r:r + 1, :], (PAIR // blk, blk, W))
                return -jnp.abs(b - ref.reshape(PAIR, W))

            ex = [jnp.exp(gap_to_row(2 * s, s - 1 if d == 0 else s)) for s in (CHUNK >> (lv + 1) for lv in range(VPU_LEVELS))]
            ex += [jnp.exp(z[j * PAIR:(j + 1) * PAIR]) for j in range(N_LEVELS - VPU_LEVELS)]
            lvl.append(ex)
            e_cum.append(heads(jnp.exp(b)))
            e_rest.append(heads(jnp.exp(gap_to_row(CHUNK, CHUNK - 1 if d == 0 else 0))))
            qh.append(heads(q_s[pl.ds(rows[d], PAIR), :]))
            vh.append(heads(v_s[pl.ds(rows[d], PAIR), :]))
        att = [jnp.where(eye, jnp.sum(qh[d] * kh[d], axis=-1, keepdims=True), 0.0) for d in dirs]
        for d in dirs:
            for lv in range(N_LEVELS):
                rr = (jnp.where(role_ref[d, lv] > 0.5, qh[d], kh[d]) * heads(lvl[d][lv])).astype(BF16)
                att[d] = att[d] + _bdot_nt(rr, rr) * msk_ref[d, lv]
        intra = [_bdot(att[d].astype(BF16), vh[d]) for d in dirs]
        qe = [(qh[d] * e_cum[d]).astype(BF16) for d in dirs]
        kd = [(kh[d] * e_rest[d]).astype(BF16) for d in dirs]
        st = [st_s[d * HEADS:(d + 1) * HEADS] for d in dirs]
        outs = [[None, None], [None, None]]
        for visit in range(2):
            for d in dirs:
                c = visit if d == 0 else 1 - visit
                cs = slice(c * CHUNK, (c + 1) * CHUNK)
                last = (c + 1) * CHUNK - 1 if d == 0 else c * CHUNK
                outs[d][c] = _bdot_nt(qe[d][:, cs], st[d].astype(BF16)) + intra[d][:, cs]
                st[d] = st[d] * e_cum[d][:, last:last + 1] + _bdot_tn(vh[d][:, cs], kd[d][:, cs])
        for d in dirs:
            st_s[d * HEADS:(d + 1) * HEADS] = st[d]
            for h in range(HEADS):
                o_s[pl.ds(rows[d], PAIR), h * HEAD_DIM:(h + 1) * HEAD_DIM] += jnp.concatenate(
                    [outs[d][0][h], outs[d][1][h]], axis=0)
        return carry

    lax.fori_loop(0, npc + npl, scan, 0)

    def gate_of(r0, blk):
        return _dot(_seq_rows(hc_ref, hl_ref, r0 // blk, blk, L), w_ref[:, 4 * W:5 * W])

    _head_norm_store(o_s, gate_of, gn_ref[...], (oc_ref, ol_ref), (L, T), R)


def _seq_rows(hc_ref, hl_ref, i, blk, L):
    assert L == blk
    r0 = pl.multiple_of(jnp.maximum(i - 1, 0) * blk, blk)
    return jnp.where(i == 0, hc_ref[...], hl_ref[pl.ds(r0, blk), :])


def _hgrn_call(hc, hl, w_c, c_lb, gn, pm, role, msk, *, layer, B, T, L):
    d = hl.shape[1]
    TL = T + L
    const = lambda shape: pl.BlockSpec(shape, lambda b: (0,) * len(shape))
    return pl.pallas_call(
        functools.partial(_hgrn_kernel, T=T, L=L, layer=layer),
        grid=(B,),
        in_specs=[
            pl.BlockSpec((L, d), lambda b: (b, 0)),
            pl.BlockSpec((T, d), lambda b: (b, 0)),
            pl.BlockSpec((None, d, 5 * MIX_HW), lambda b: (layer, 0, 0)),
            const(c_lb.shape),
            pl.BlockSpec((None, 1, HEAD_DIM), lambda b: (layer, 0, 0)),
            const(pm.shape), const(role.shape), const(msk.shape),
        ],
        out_specs=[pl.BlockSpec((L, MIX_HW), lambda b: (b, 0)), pl.BlockSpec((T, MIX_HW), lambda b: (b, 0))],
        out_shape=[jax.ShapeDtypeStruct((B * L, MIX_HW), BF16), jax.ShapeDtypeStruct((B * T, MIX_HW), BF16)],
        scratch_shapes=[
            pltpu.VMEM((TL, MIX_HW), F32),
            pltpu.VMEM((TL, MIX_HW), BF16),
            pltpu.VMEM((2, TL, MIX_HW), F32),
            pltpu.VMEM((TL, MIX_HW), F32),
            pltpu.VMEM((2 * HEADS, HEAD_DIM, HEAD_DIM), F32),
        ],
        compiler_params=_params(("parallel",)),
    )(hc, hl, w_c, c_lb, gn, pm, role, msk)


def _tri_tables():
    i = np.arange(PAIR)[:, None]
    t = np.arange(PAIR)[None, :]
    same = (i // CHUNK) == (t // CHUNK)
    return np.stack([same & (t <= i), same & (t >= i)]).astype(np.float32)


def _hgrn_tables():
    i = np.arange(PAIR)
    n_mat = N_LEVELS - VPU_LEVELS
    pm = np.zeros((2, n_mat + 1, PAIR, PAIR), np.float32)
    role = np.zeros((2, N_LEVELS, PAIR, 1), np.float32)
    msk = np.zeros((2, N_LEVELS, PAIR, PAIR), np.float32)
    same_chunk = (i[:, None] // CHUNK) == (i[None, :] // CHUNK)
    for d in range(2):
        for lv in range(N_LEVELS):
            s = CHUNK >> (lv + 1)
            mid = (i // (2 * s)) * (2 * s) + s
            is_q = (i >= mid) if d == 0 else (i < mid)
            role[d, lv, :, 0] = is_q
            for r in range(PAIR if lv >= VPU_LEVELS else 0):
                m = mid[r]
                if d == 0:
                    lo_t, hi_t = (m, r) if is_q[r] else (r + 1, m - 1)
                else:
                    lo_t, hi_t = (r, m - 1) if is_q[r] else (m, r - 1)
                pm[d, lv - VPU_LEVELS, r, lo_t:hi_t + 1] = 1.0
            same_blk = (i[:, None] // (2 * s)) == (i[None, :] // (2 * s))
            msk[d, lv] = same_blk & is_q[:, None] & ~is_q[None, :]
        t = i[None, :]
        r = i[:, None]
        pm[d, n_mat] = same_chunk & ((t <= r) if d == 0 else (t >= r))
    role = np.broadcast_to(role, (2, N_LEVELS, PAIR, HEAD_DIM)).copy()
    return pm.reshape(2, (n_mat + 1) * PAIR, PAIR), role, msk


def kernel(x, c, ctx, c_ctx, w_ada, b_ada, norm_g, w_ffn_gu, w_ffn_d, w_in, w_branch, w_out, a_qk_norm, a_sink, b_conv, b_a_log, b_dt_bias, b_norm, c_lb, c_norm):
    B, T, D = x.shape
    L = ctx.shape[1]
    depth = w_ada.shape[0]
    assert T % A_QBLK == 0 and L == 256 and D % LANES == 0

    n_rows = -(-(B + 1) // 8) * 8
    c_all = jnp.zeros((n_rows, D), F32).at[:B].set(c).at[B].set(c_ctx)
    mods = jnp.swapaxes(_ada_call(c_all, w_ada, b_ada), 1, 2)

    bf = lambda a: a.astype(BF16)
    w_gu, w_d = bf(w_ffn_gu), bf(w_ffn_d)
    o_qkv = A_Q_W + 2 * A_KV_W
    o_b = o_qkv + 4 * MIX_HW
    o_ba = o_b + 4 * HEADS
    o_c = o_ba + 5 * MIX_HW
    w_a = bf(w_in[:, :, :o_qkv])
    w_b = bf(jnp.concatenate([w_in[:, :, o_qkv:o_b],
                              jnp.pad(w_in[:, :, o_b:o_ba], ((0, 0), (0, 0), (0, LANES - 4 * HEADS)))], axis=-1))
    w_c = bf(w_in[:, :, o_ba:o_c])
    w_gate = bf(w_in[:, :, o_c:])
    w_br, w_o = bf(w_branch), bf(w_out)
    norm_g4 = norm_g.reshape(depth, 3, 1, D)

    gq = jnp.tile(a_qk_norm[:, 0:1, :], (1, 1, LANES // A_HEAD_DIM)) * (A_HEAD_DIM ** -0.5)
    gk = jnp.tile(a_qk_norm[:, 1:2, :], (1, 1, LANES // A_HEAD_DIM))
    sink = jnp.broadcast_to(a_sink[:, :, None], (depth, A_HEADS, LANES))
    pad_ba = lambda a: jnp.pad(a.reshape(depth, 1, 2 * HEADS), ((0, 0), (0, 0), (8, LANES - 8 - 2 * HEADS)))
    nalog = pad_ba(-jnp.exp(b_a_log))
    dtb = pad_ba(b_dt_bias)
    cos_t, sin_t = _rope_tables(T)
    bd = jnp.asarray(np.kron(np.eye(LANES // A_HEAD_DIM), np.ones((A_HEAD_DIM, A_HEAD_DIM))), BF16)
    tri = jnp.asarray(_tri_tables(), BF16)
    pm_np, role_np, msk_np = _hgrn_tables()
    pm, role, msk = jnp.asarray(pm_np, BF16), jnp.asarray(role_np, F32), jnp.asarray(msk_np, F32)

    xl = x.reshape(B * T, D)
    xc = ctx.reshape(B * L, D)
    tm_l, tm_c = min(FFN_ROWS, T), min(FFN_ROWS, B * L)
    tm_ml, tm_mc = min(MERGE_ROWS, T), min(MERGE_ROWS, B * L)
    row_l = lambda tm: (lambda i: (i * tm) // T)
    row_c = lambda tm: (lambda i: B)
    tf = FFN_COLS

    for l in range(depth):
        last = l == depth - 1
        ffn = functools.partial(_ffn_call, mods=mods, norm_g=norm_g4, w_gu=w_gu, w_d=w_d, layer=l, tf=tf)
        xl, hl = ffn(xl, j_ffn=0, j_norm=0, mod_row=row_l(tm_l), emit_h=True, tm=tm_l)
        xc, hc = ffn(xc, j_ffn=0, j_norm=0, mod_row=row_c(tm_c), emit_h=True, tm=tm_c)
        a_c, a_l = _attn_call(hc, hl, w_a, gq, gk, sink, cos_t, sin_t, bd, layer=l, B=B, T=T, L=L)
        b_c, b_l = _delta_call(hc, hl, w_b, b_conv, nalog, dtb, b_norm.reshape(depth, 1, HEAD_DIM), tri,
                               layer=l, B=B, T=T, L=L)
        c_c, c_l = _hgrn_call(hc, hl, w_c, c_lb, c_norm.reshape(depth, 1, HEAD_DIM), pm, role, msk,
                              layer=l, B=B, T=T, L=L)
        merge = functools.partial(_merge_call, mods=mods, w_gate=w_gate, w_branch=w_br, w_out=w_o, layer=l)
        xl = merge(xl, hl, a_l, b_l, c_l, mod_row=row_l(tm_ml), tm=tm_ml)
        xl = ffn(xl, j_ffn=1, j_norm=2, mod_row=row_l(tm_l), emit_h=False, tm=tm_l)
        if not last:
            xc = merge(xc, hc, a_c, b_c, c_c, mod_row=row_c(tm_mc), tm=tm_mc)
            xc = ffn(xc, j_ffn=1, j_norm=2, mod_row=row_c(tm_c), emit_h=False, tm=tm_c)
    return xl.reshape(B, T, D)
```
